```python
import jax, jax.numpy as jnp
from jax import lax
import numpy as np

D_MODEL = 1024
BATCH = 8
SEQ = 4096
DEPTH = 2

GRID_W = 64
CTX_LEN = 256
N_EVEN = (DEPTH + 1) // 2
N_ODD = DEPTH // 2
EPS = 1e-6
A_CH = D_MODEL // 2
CONV_K = 31
B_CH = D_MODEL // 2
B_HEADS = 8
B_HD = B_CH // B_HEADS
CHUNK = 128
IN_E = 2 * A_CH + 2 * B_CH
QK_NOPE = 128
QK_ROPE = 64
V_HD = 128
MLA_HEADS = D_MODEL // V_HD
Q_RANK = 384
KV_RANK = 256
IN_O = Q_RANK + KV_RANK + QK_ROPE
ROPE_AXIS = QK_ROPE // 2
ROPE_FREQS = ROPE_AXIS // 2
ROPE_BASE = 10000.0
Q_BLOCK = 128
ATTN_SCALE = (QK_NOPE + QK_ROPE) ** -0.5
D_FF = 2816
N_EXPERTS = 8
TOP_K = 2
D_EXPERT = 3584

kernel_name = 'hybrid_conv_gmlp_mla_moe_dit'


def _rmsnorm(x, g):
    xf = x.astype(jnp.float32)
    y = xf * lax.rsqrt(jnp.mean(xf * xf, axis=-1, keepdims=True) + EPS)
    return (y * g.astype(jnp.float32)).astype(x.dtype)


def _layernorm(x, g, b):
    xf = x.astype(jnp.float32)
    mu = jnp.mean(xf, axis=-1, keepdims=True)
    var = jnp.mean(jnp.square(xf - mu), axis=-1, keepdims=True)
    y = (xf - mu) * lax.rsqrt(var + EPS)
    return (y * g.astype(jnp.float32) + b.astype(jnp.float32)).astype(x.dtype)


def _modulate(h, shift, scale):
    return h * (1 + scale[:, None, :]) + shift[:, None, :]


def _swiglu(h, w1, w3, w2):
    return (jax.nn.silu(h @ w1) * (h @ w3)) @ w2


def _rope2d(x, cos, sin):
    xs = x.reshape(*x.shape[:-1], 2, 2, ROPE_FREQS)
    x1, x2 = xs[..., 0, :], xs[..., 1, :]
    c = cos[None, :, None].astype(x.dtype)
    s = sin[None, :, None].astype(x.dtype)
    out = jnp.stack([x1 * c - x2 * s, x1 * s + x2 * c], axis=-2)
    return out.reshape(x.shape)


def _conv_gmlp_mixer(h, w_in, conv_w, conv_b, ln_a_g, ln_a_b, ln_v_g, ln_v_b, w_s, b_s, w_out):
    b, t, _ = h.shape
    z = h @ w_in
    a_val, a_gate, u, v = jnp.split(z, 4, axis=-1)
    a = a_val * jax.nn.sigmoid(a_gate)
    a = lax.conv_general_dilated(
        a, conv_w.reshape(CONV_K, 1, A_CH).astype(a.dtype), window_strides=(1,),
        padding=[(CONV_K // 2, CONV_K // 2)], dimension_numbers=('NWC', 'WIO', 'NWC'),
        feature_group_count=A_CH) + conv_b
    a = jax.nn.silu(_layernorm(a, ln_a_g, ln_a_b))
    u = jax.nn.gelu(u)
    v = _layernorm(jax.nn.gelu(v), ln_v_g, ln_v_b).reshape(b, t // CHUNK, CHUNK, B_HEADS, B_HD)
    sv = jnp.einsum('hij,bnjhd->bnihd', w_s, v) + b_s.T[None, None, :, :, None]
    bo = u * sv.reshape(b, t, B_CH)
    return jnp.concatenate([a, bo], axis=-1) @ w_out


def _mla_q(q_lat, g_q, w_uq, rope):
    b, t, _ = q_lat.shape
    q = (_rmsnorm(q_lat, g_q) @ w_uq).reshape(b, t, MLA_HEADS, QK_NOPE + QK_ROPE)
    if rope is not None:
        q = jnp.concatenate([q[..., :QK_NOPE], _rope2d(q[..., QK_NOPE:], *rope)], axis=-1)
    return q


def _mla_kv(kv_lat, k_pe, g_kv, w_ukv, rope):
    b, t, _ = kv_lat.shape
    kv = (_rmsnorm(kv_lat, g_kv) @ w_ukv).reshape(b, t, MLA_HEADS, QK_NOPE + V_HD)
    k_pe = k_pe[:, :, None, :]
    if rope is not None:
        k_pe = _rope2d(k_pe, *rope)
    k = jnp.concatenate([kv[..., :QK_NOPE], jnp.broadcast_to(k_pe, (b, t, MLA_HEADS, QK_ROPE))], axis=-1)
    return k, kv[..., QK_NOPE:]


def _attend(q, k, v):
    b, tq, h, dq = q.shape
    nb = tq // Q_BLOCK
    qb = jnp.moveaxis(q.reshape(b, nb, Q_BLOCK, h, dq), 1, 0)

    def block(qblk):
        s = jnp.einsum('bqhd,bkhd->bhqk', qblk, k).astype(jnp.float32) * ATTN_SCALE
        p = jax.nn.softmax(s, axis=-1).astype(v.dtype)
        return jnp.einsum('bhqk,bkhd->bqhd', p, v)

    o = lax.map(block, qb)
    return jnp.moveaxis(o, 0, 1).reshape(b, tq, h * v.shape[-1])


def _moe(h, router, w1, w3, w2):
    b, t, d = h.shape
    hf = h.reshape(b * t, d)
    logits = (hf @ router).astype(jnp.float32)
    top_v, top_i = lax.top_k(logits, TOP_K)
    top_w = jax.nn.softmax(top_v, axis=-1)
    gates = jnp.sum(jax.nn.one_hot(top_i, N_EXPERTS, dtype=jnp.float32) * top_w[..., None], axis=-2).astype(h.dtype)
    y = jnp.zeros_like(hf)
    for e in range(N_EXPERTS):
        y = y + gates[:, e:e + 1] * _swiglu(hf, w1[e], w3[e], w2[e])
    return y.reshape(b, t, d)


def setup_inputs(seed: int = 0) -> dict:
    key = jax.random.key(seed)
    ks = iter(jax.random.split(key, 64))

    def nrm(shape, s=1.0):
        return jax.random.normal(next(ks), shape, jnp.float32) * s

    def gain(shape):
        return 1.0 + nrm(shape, 0.02)

    D = D_MODEL
    return {
        'x': nrm((BATCH, SEQ, D)),
        'c': nrm((BATCH, D)),
        'ctx': nrm((BATCH, CTX_LEN, D)),
        'c_ctx': nrm((D,)),
        'w_ada': nrm((DEPTH, D, 6 * D), 0.5 * D ** -0.5),
        'b_ada': nrm((DEPTH, 6 * D), 0.02),
        'norm_g': gain((DEPTH, 2, D)),
        'e_w_in': nrm((N_EVEN, D, IN_E), D ** -0.5),
        'e_conv_w': nrm((N_EVEN, CONV_K, A_CH), CONV_K ** -0.5),
        'e_conv_b': nrm((N_EVEN, A_CH), 0.02),
        'e_ln_a_g': gain((N_EVEN, A_CH)),
        'e_ln_a_b': nrm((N_EVEN, A_CH), 0.02),
        'e_ln_v_g': gain((N_EVEN, B_CH)),
        'e_ln_v_b': nrm((N_EVEN, B_CH), 0.02),
        'e_w_s': nrm((N_EVEN, B_HEADS, CHUNK, CHUNK), CHUNK ** -0.5),
        'e_b_s': gain((N_EVEN, B_HEADS, CHUNK)),
        'e_w_out': nrm((N_EVEN, A_CH + B_CH, D), (A_CH + B_CH) ** -0.5),
        'e_ffn_w1': nrm((N_EVEN, D, D_FF), D ** -0.5),
        'e_ffn_w3': nrm((N_EVEN, D, D_FF), D ** -0.5),
        'e_ffn_w2': nrm((N_EVEN, D_FF, D), D_FF ** -0.5),
        'o_w_in': nrm((N_ODD, D, IN_O), D ** -0.5),
        'o_g_q': gain((N_ODD, Q_RANK)),
        'o_g_kv': gain((N_ODD, KV_RANK)),
        'o_w_uq': nrm((N_ODD, Q_RANK, MLA_HEADS * (QK_NOPE + QK_ROPE)), Q_RANK ** -0.5),
        'o_w_ukv': nrm((N_ODD, KV_RANK, MLA_HEADS * (QK_NOPE + V_HD)), KV_RANK ** -0.5),
        'o_w_o': nrm((N_ODD, MLA_HEADS * V_HD, D), (MLA_HEADS * V_HD) ** -0.5),
        'o_router': nrm((N_ODD, D, N_EXPERTS), D ** -0.5),
        'o_exp_w1': nrm((N_ODD, N_EXPERTS, D, D_EXPERT), D ** -0.5),
        'o_exp_w3': nrm((N_ODD, N_EXPERTS, D, D_EXPERT), D ** -0.5),
        'o_exp_w2': nrm((N_ODD, N_EXPERTS, D_EXPERT, D), D_EXPERT ** -0.5),
        'final_g': gain((D,)),
    }


def reference(x, c, ctx, c_ctx, w_ada, b_ada, norm_g,
              e_w_in, e_conv_w, e_conv_b, e_ln_a_g, e_ln_a_b, e_ln_v_g, e_ln_v_b,
              e_w_s, e_b_s, e_w_out, e_ffn_w1, e_ffn_w3, e_ffn_w2,
              o_w_in, o_g_q, o_g_kv, o_w_uq, o_w_ukv, o_w_o,
              o_router, o_exp_w1, o_exp_w3, o_exp_w2, final_g):
    b, t, _ = x.shape
    rows = t // GRID_W
    row = jnp.repeat(jnp.arange(rows), GRID_W).astype(jnp.float32)
    col = jnp.tile(jnp.arange(GRID_W), rows).astype(jnp.float32)
    inv_freq = ROPE_BASE ** (-jnp.arange(ROPE_FREQS, dtype=jnp.float32) / ROPE_FREQS)
    ang = jnp.stack([row[:, None] * inv_freq, col[:, None] * inv_freq], axis=1)
    rope = (jnp.cos(ang), jnp.sin(ang))

    silu_c = jax.nn.silu(c)
    silu_cc = jax.nn.silu(c_ctx)[None]

    for i in range(DEPTH):
        last = i == DEPTH - 1
        j = i // 2
        mx = jnp.split(silu_c @ w_ada[i] + b_ada[i], 6, axis=-1)
        mc = jnp.split(silu_cc @ w_ada[i] + b_ada[i], 6, axis=-1)
        hx = _modulate(_rmsnorm(x, norm_g[i, 0]), mx[0], mx[1])
        hc = _modulate(_rmsnorm(ctx, norm_g[i, 0]), mc[0], mc[1])

        if i % 2 == 0:
            params = (e_w_in[j], e_conv_w[j], e_conv_b[j], e_ln_a_g[j], e_ln_a_b[j],
                      e_ln_v_g[j], e_ln_v_b[j], e_w_s[j], e_b_s[j], e_w_out[j])
            x = x + mx[2][:, None] * _conv_gmlp_mixer(hx, *params)
            if not last:
                ctx = ctx + mc[2][:, None] * _conv_gmlp_mixer(hc, *params)
        else:
            zx = hx @ o_w_in[j]
            q_lat_x, kv_lat_x, kpe_x = jnp.split(zx, [Q_RANK, Q_RANK + KV_RANK], axis=-1)
            if last:
                zc = hc @ o_w_in[j][:, Q_RANK:]
                kv_lat_c, kpe_c = jnp.split(zc, [KV_RANK], axis=-1)
            else:
                zc = hc @ o_w_in[j]
                q_lat_c, kv_lat_c, kpe_c = jnp.split(zc, [Q_RANK, Q_RANK + KV_RANK], axis=-1)
            k_c, v_c = _mla_kv(kv_lat_c, kpe_c, o_g_kv[j], o_w_ukv[j], None)
            k_x, v_x = _mla_kv(kv_lat_x, kpe_x, o_g_kv[j], o_w_ukv[j], rope)
            q_x = _mla_q(q_lat_x, o_g_q[j], o_w_uq[j], rope)
            att_x = _attend(q_x, jnp.concatenate([k_c, k_x], axis=1), jnp.concatenate([v_c, v_x], axis=1))
            x = x + mx[2][:, None] * (att_x @ o_w_o[j])
            if not last:
                q_c = _mla_q(q_lat_c, o_g_q[j], o_w_uq[j], None)
                att_c = _attend(q_c, k_c, v_c)
                ctx = ctx + mc[2][:, None] * (att_c @ o_w_o[j])

        if i % 2 == 0:
            ffn = lambda h: _swiglu(h, e_ffn_w1[j], e_ffn_w3[j], e_ffn_w2[j])
        else:
            ffn = lambda h: _moe(h, o_router[j], o_exp_w1[j], o_exp_w3[j], o_exp_w2[j])
        hx = _modulate(_rmsnorm(x, norm_g[i, 1]), mx[3], mx[4])
        x = x + mx[5][:, None] * ffn(hx)
        if not last:
            hc = _modulate(_rmsnorm(ctx, norm_g[i, 1]), mc[3], mc[4])
            ctx = ctx + mc[5][:, None] * ffn(hc)

    return _rmsnorm(x, final_g)
```

```python
import functools

import jax
import jax.numpy as jnp
from jax import lax
from jax.experimental import pallas as pl
from jax.experimental.pallas import tpu as pltpu

F32 = jnp.float32
BF16 = jnp.bfloat16

EPS = 1e-6
GRID_W = 64
CONV_K = 31
CHUNK = 128
B_HEADS = 8
QK_NOPE = 128
QK_ROPE = 64
V_HD = 128
Q_RANK = 384
KV_RANK = 256
ROPE_BASE = 10000.0
N_EXPERTS = 8
ATTN_SCALE = (QK_NOPE + QK_ROPE) ** -0.5

LANE = 128
HALO = 16
CONV_ROWS = 64
QK_PAD = 256
VMEM_LIMIT = 52 * 1024 * 1024


def _cparams(sem):
    return pltpu.CompilerParams(dimension_semantics=sem, vmem_limit_bytes=VMEM_LIMIT)


def _silu(x):
    return x * jax.nn.sigmoid(x)


def _rms_mod(x, g, shift, scale):
    y = x * lax.rsqrt(jnp.mean(x * x, axis=-1, keepdims=True) + EPS) * g
    return y * (1.0 + scale) + shift


def _rms(x, g):
    return x * lax.rsqrt(jnp.mean(x * x, axis=-1, keepdims=True) + EPS) * g


def _dot(a, b):
    return jnp.dot(a, b, preferred_element_type=F32)


def _ada_kernel(cc_ref, w_ref, b_ref, o_ref):
    s = _silu(cc_ref[...]).astype(BF16)
    o_ref[0] = _dot(s, w_ref[0].astype(BF16)) + b_ref[0]


def _ada(cc, w_ada, b_ada):
    depth, d, d6 = w_ada.shape
    rows = cc.shape[0]
    tn = 1536
    return pl.pallas_call(
        _ada_kernel,
        grid=(depth, d6 // tn),
        in_specs=[
            pl.BlockSpec((rows, d), lambda i, j: (0, 0)),
            pl.BlockSpec((1, d, tn), lambda i, j: (i, 0, j)),
            pl.BlockSpec((1, 1, tn), lambda i, j: (i, 0, j)),
        ],
        out_specs=pl.BlockSpec((1, rows, tn), lambda i, j: (i, 0, j)),
        out_shape=jax.ShapeDtypeStruct((depth, rows, d6), F32),
        compiler_params=_cparams(("parallel", "parallel")),
        name="ada",
    )(cc, w_ada, b_ada.reshape(depth, 1, d6))


def _inproj_kernel(x_ref, mod_ref, g_ref, w_ref, lng_ref, lnb_ref, a_ref, u_ref, v_ref):
    ch = a_ref.shape[1]
    h = _rms_mod(x_ref[...], g_ref[...], mod_ref[0, 0:1, :], mod_ref[0, 1:2, :]).astype(BF16)
    z = _dot(h, w_ref[...])
    a = z[:, 0:ch] * jax.nn.sigmoid(z[:, ch:2 * ch])
    u = jax.nn.gelu(z[:, 2 * ch:3 * ch])
    v = jax.nn.gelu(z[:, 3 * ch:4 * ch])
    mu = jnp.mean(v, axis=-1, keepdims=True)
    vc = v - mu
    var = jnp.mean(vc * vc, axis=-1, keepdims=True)
    v = vc * lax.rsqrt(var + EPS) * lng_ref[...] + lnb_ref[...]
    a_ref[...] = a.astype(BF16)
    u_ref[...] = u.astype(BF16)
    v_ref[...] = v.astype(BF16)


def _inproj(x2, mod, mod_map, g, w_in, ln_g, ln_b, tm):
    n, d = x2.shape
    ch = w_in.shape[1] // 4
    row = lambda t: (t, 0)
    const = lambda t: (0, 0)
    out = jax.ShapeDtypeStruct((n, ch), BF16)
    return pl.pallas_call(
        _inproj_kernel,
        grid=(n // tm,),
        in_specs=[
            pl.BlockSpec((tm, d), row),
            pl.BlockSpec((1, 3, d), mod_map),
            pl.BlockSpec((1, d), const),
            pl.BlockSpec(w_in.shape, const),
            pl.BlockSpec((1, ch), const),
            pl.BlockSpec((1, ch), const),
        ],
        out_specs=[pl.BlockSpec((tm, ch), row)] * 3,
        out_shape=[out, out, out],
        compiler_params=_cparams(("parallel",)),
        name="l0_inproj",
    )(x2, mod, g, w_in, ln_g, ln_b)


def _mixer_kernel(a_ref, ap_ref, an_ref, u_ref, v_ref, x_ref, mod_ref, cw_ref, cb_ref,
                  lag_ref, lab_ref, ws_ref, bs_ref, wo_ref, o_ref,
                  ext_ref, sh_ref, cat_ref, *, tiles_per_seq):
    tm, ch = a_ref.shape
    t = pl.program_id(0)
    first = (t % tiles_per_seq) == 0
    last = (t % tiles_per_seq) == tiles_per_seq - 1

    ext_ref[0:HALO, :] = jnp.where(first, 0.0, ap_ref[...].astype(F32))
    ext_ref[HALO:HALO + tm, :] = a_ref[...].astype(F32)
    ext_ref[HALO + tm:, :] = jnp.where(last, 0.0, an_ref[...].astype(F32))
    span = tm + 24
    for r in range(1, 8):
        sh_ref[r - 1] = ext_ref[r:r + span, :]

    off0 = HALO - CONV_K // 2
    cb = cb_ref[...]
    lag = lag_ref[...]
    lab = lab_ref[...]

    def conv_block(rb, carry):
        base = pl.multiple_of(rb * CONV_ROWS, CONV_ROWS)
        acc = jnp.broadcast_to(cb, (CONV_ROWS, ch))
        for k in range(CONV_K):
            q, r = divmod(off0 + k, 8)
            start = pl.multiple_of(base + 8 * q, 8)
            if r == 0:
                win = ext_ref[pl.ds(start, CONV_ROWS), :]
            else:
                win = sh_ref[r - 1, pl.ds(start, CONV_ROWS), :]
            acc = acc + cw_ref[k:k + 1, :] * win
        mu = jnp.mean(acc, axis=-1, keepdims=True)
        c = acc - mu
        var = jnp.mean(c * c, axis=-1, keepdims=True)
        y = c * lax.rsqrt(var + EPS) * lag + lab
        cat_ref[pl.ds(base, CONV_ROWS), 0:ch] = _silu(y).astype(BF16)
        return carry

    lax.fori_loop(0, tm // CONV_ROWS, conv_block, 0)

    lane = lax.broadcasted_iota(jnp.int32, (CHUNK, LANE), 1)
    lo_half = lane < (LANE // 2)
    for c in range(tm // CHUNK):
        rows = slice(c * CHUNK, (c + 1) * CHUNK)
        for p in range(ch // LANE):
            cols = slice(p * LANE, (p + 1) * LANE)
            vv = v_ref[rows, cols]
            s0 = _dot(ws_ref[2 * p], vv)
            s1 = _dot(ws_ref[2 * p + 1], vv)
            sv = jnp.where(lo_half, s0, s1) + bs_ref[:, cols]
            bo = u_ref[rows, cols].astype(F32) * sv
            cat_ref[rows, ch + p * LANE:ch + (p + 1) * LANE] = bo.astype(BF16)

    y = _dot(cat_ref[...], wo_ref[...])
    o_ref[...] = x_ref[...] + mod_ref[0, 2:3, :] * y


def _mixer(a, u, v, x2, mod, mod_map, conv_w, conv_b, la_g, la_b, w_s, bias_map, w_out, tm, seq):
    n, d = x2.shape
    ch = a.shape[1]
    hb = tm // HALO
    nhb = n // HALO
    row = lambda t: (t, 0)
    const = lambda t: (0, 0)
    kern = functools.partial(_mixer_kernel, tiles_per_seq=seq // tm)
    return pl.pallas_call(
        kern,
        grid=(n // tm,),
        in_specs=[
            pl.BlockSpec((tm, ch), row),
            pl.BlockSpec((HALO, ch), lambda t: (jnp.maximum(t * hb - 1, 0), 0)),
            pl.BlockSpec((HALO, ch), lambda t: (jnp.minimum((t + 1) * hb, nhb - 1), 0)),
            pl.BlockSpec((tm, ch), row),
            pl.BlockSpec((tm, ch), row),
            pl.BlockSpec((tm, d), row),
            pl.BlockSpec((1, 3, d), mod_map),
            pl.BlockSpec(conv_w.shape, const),
            pl.BlockSpec((1, ch), const),
            pl.BlockSpec((1, ch), const),
            pl.BlockSpec((1, ch), const),
            pl.BlockSpec(w_s.shape, lambda t: (0, 0, 0)),
            pl.BlockSpec(bias_map.shape, const),
            pl.BlockSpec(w_out.shape, const),
        ],
        out_specs=pl.BlockSpec((tm, d), row),
        out_shape=jax.ShapeDtypeStruct((n, d), F32),
        scratch_shapes=[
            pltpu.VMEM((tm + 2 * HALO, ch), F32),
            pltpu.VMEM((7, tm + 24, ch), F32),
            pltpu.VMEM((tm, 2 * ch), BF16),
        ],
        compiler_params=_cparams(("parallel",)),
        name="l0_mixer",
    )(a, a, a, u, v, x2, mod, conv_w, conv_b, la_g, la_b, w_s, bias_map, w_out)


def _ffn_kernel(*refs, moe, final):
    x_ref, mod_ref, g_ref = refs[0:3]
    k = 3
    gates_ref = None
    if moe:
        gates_ref = refs[k]
        k += 1
    w1_ref, w3_ref, w2_ref = refs[k:k + 3]
    k += 3
    fg_ref = None
    if final:
        fg_ref = refs[k]
        k += 1
    o_ref, h_ref, acc_ref = refs[k:k + 3]

    e = pl.program_id(1)
    f = pl.program_id(2)
    last = (e == pl.num_programs(1) - 1) & (f == pl.num_programs(2) - 1)

    @pl.when((e == 0) & (f == 0))
    def _():
        h = _rms_mod(x_ref[...], g_ref[...], mod_ref[0, 0:1, :], mod_ref[0, 1:2, :])
        h_ref[...] = h.astype(BF16)
        acc_ref[...] = jnp.zeros_like(acc_ref)

    h = h_ref[...]
    h1 = _dot(h, w1_ref[0])
    h3 = _dot(h, w3_ref[0])
    act = _silu(h1) * h3
    if moe:
        gates = gates_ref[...]
        lane = lax.broadcasted_iota(jnp.int32, gates.shape, 1)
        act = act * jnp.sum(jnp.where(lane == e, gates, 0.0), axis=-1, keepdims=True)
    acc_ref[...] += _dot(act.astype(BF16), w2_ref[0])

    @pl.when(last)
    def _():
        y = x_ref[...] + mod_ref[0, 2:3, :] * acc_ref[...]
        if final:
            y = _rms(y, fg_ref[...])
        o_ref[...] = y


def _ffn(x2, mod, mod_map, g, w1, w3, w2, tm, tf, gates=None, final_g=None):
    n, d = x2.shape
    ne, _, dff = w1.shape
    moe = gates is not None
    final = final_g is not None
    row = lambda t, e, f: (t, 0)
    const = lambda t, e, f: (0, 0)
    in_specs = [
        pl.BlockSpec((tm, d), row),
        pl.BlockSpec((1, 3, d), lambda t, e, f: mod_map(t)),
        pl.BlockSpec((1, d), const),
    ]
    args = [x2, mod, g]
    if moe:
        in_specs.append(pl.BlockSpec((tm, LANE), row))
        args.append(gates)
    in_specs += [
        pl.BlockSpec((1, d, tf), lambda t, e, f: (e, 0, f)),
        pl.BlockSpec((1, d, tf), lambda t, e, f: (e, 0, f)),
        pl.BlockSpec((1, tf, d), lambda t, e, f: (e, f, 0)),
    ]
    args += [w1, w3, w2]
    if final:
        in_specs.append(pl.BlockSpec((1, d), const))
        args.append(final_g)
    return pl.pallas_call(
        functools.partial(_ffn_kernel, moe=moe, final=final),
        grid=(n // tm, ne, dff // tf),
        in_specs=in_specs,
        out_specs=pl.BlockSpec((tm, d), row),
        out_shape=jax.ShapeDtypeStruct((n, d), F32),
        scratch_shapes=[pltpu.VMEM((tm, d), BF16), pltpu.VMEM((tm, d), F32)],
        compiler_params=_cparams(("parallel", "arbitrary", "arbitrary")),
        name="moe_dense" if moe else "ffn",
    )(*args)


def _mla_proj_kernel(x_ref, mod_ref, g_ref, wq_ref, wkv_ref, wpe_ref, gq_ref, gkv_ref,
                     wa_ref, wb_ref, wukv_ref, cos_ref, sin_ref,
                     q_ref, k_ref, v_ref):
    heads = q_ref.shape[1]
    h = _rms_mod(x_ref[...], g_ref[...], mod_ref[0, 0:1, :], mod_ref[0, 1:2, :]).astype(BF16)
    cos = cos_ref[...]
    sin = sin_ref[...]
    zpe = _dot(h, wpe_ref[...])
    kr = (zpe[:, 0:LANE] * cos + zpe[:, LANE:2 * LANE] * sin).astype(BF16)

    qn = _rms(_dot(h, wq_ref[...]), gq_ref[...]).astype(BF16)
    qa = _dot(qn, wa_ref[...])
    qb = _dot(qn, wb_ref[...])
    kvn = _rms(_dot(h, wkv_ref[...]), gkv_ref[...]).astype(BF16)
    kv = _dot(kvn, wukv_ref[...])
    for hd in range(heads):
        q_ref[0, hd, :, 0:LANE] = (qa[:, hd * QK_PAD:hd * QK_PAD + LANE] * ATTN_SCALE).astype(BF16)
        qr = (qa[:, hd * QK_PAD + LANE:(hd + 1) * QK_PAD] * cos
              + qb[:, hd * LANE:(hd + 1) * LANE] * sin)
        q_ref[0, hd, :, LANE:QK_PAD] = (qr * ATTN_SCALE).astype(BF16)
        k_ref[0, hd, :, 0:LANE] = kv[:, hd * 2 * LANE:hd * 2 * LANE + LANE].astype(BF16)
        k_ref[0, hd, :, LANE:QK_PAD] = kr
        v_ref[0, hd, :, :] = kv[:, hd * 2 * LANE + LANE:(hd + 1) * 2 * LANE].astype(BF16)


def _mla_proj(x3, mod, mod_map, g, wq, wkv, wpe, gq, gkv, wa, wb, wukv, cos, sin, tm):
    b, t, d = x3.shape
    heads = wukv.shape[1] // (2 * LANE)
    tps = t // tm
    row = lambda i: (i // tps, i % tps, 0)
    const = lambda i: (0, 0)
    tab = lambda i: (i % tps, 0)
    hrow = lambda i: (i // tps, 0, i % tps, 0)
    return pl.pallas_call(
        _mla_proj_kernel,
        grid=(b * tps,),
        in_specs=[
            pl.BlockSpec((None, tm, d), row),
            pl.BlockSpec((1, 3, d), mod_map),
            pl.BlockSpec((1, d), const),
            pl.BlockSpec(wq.shape, const),
            pl.BlockSpec(wkv.shape, const),
            pl.BlockSpec(wpe.shape, const),
            pl.BlockSpec(gq.shape, const),
            pl.BlockSpec(gkv.shape, const),
            pl.BlockSpec(wa.shape, const),
            pl.BlockSpec(wb.shape, const),
            pl.BlockSpec(wukv.shape, const),
            pl.BlockSpec((tm, LANE), tab),
            pl.BlockSpec((tm, LANE), tab),
        ],
        out_specs=[
            pl.BlockSpec((1, heads, tm, QK_PAD), hrow),
            pl.BlockSpec((1, heads, tm, QK_PAD), hrow),
            pl.BlockSpec((1, heads, tm, LANE), hrow),
        ],
        out_shape=[
            jax.ShapeDtypeStruct((b, heads, t, QK_PAD), BF16),
            jax.ShapeDtypeStruct((b, heads, t, QK_PAD), BF16),
            jax.ShapeDtypeStruct((b, heads, t, LANE), BF16),
        ],
        compiler_params=_cparams(("parallel",)),
        name="l1_proj",
    )(x3, mod, g, wq, wkv, wpe, gq, gkv, wa, wb, wukv, cos, sin)


def _attn_kernel(q_ref, kc_ref, vc_ref, kx_ref, vx_ref, o_ref, *, tk):
    q = q_ref[0, 0]
    tq = q.shape[0]

    def step(k, v, m, l, acc):
        s = lax.dot_general(q, k, (((1,), (1,)), ((), ())), preferred_element_type=F32)
        m_new = jnp.maximum(m, jnp.max(s, axis=-1, keepdims=True))
        alpha = jnp.exp(m - m_new)
        p = jnp.exp(s - m_new)
        l = alpha * l + jnp.sum(p, axis=-1, keepdims=True)
        acc = alpha * acc + _dot(p.astype(BF16), v)
        return m_new, l, acc

    m = jnp.full((tq, 1), -jnp.inf, F32)
    l = jnp.zeros((tq, 1), F32)
    acc = jnp.zeros((tq, vc_ref.shape[-1]), F32)
    m, l, acc = step(kc_ref[0, 0], vc_ref[0, 0], m, l, acc)
    for c in range(kx_ref.shape[2] // tk):
        m, l, acc = step(kx_ref[0, 0, c * tk:(c + 1) * tk, :], vx_ref[0, 0, c * tk:(c + 1) * tk, :],
                         m, l, acc)
    o_ref[0] = (acc / l).astype(BF16)


def _attention(q, kc, vc, kx, vx, tq, tk):
    b, heads, t, _ = q.shape
    tc = kc.shape[2]
    dv = vx.shape[-1]
    return pl.pallas_call(
        functools.partial(_attn_kernel, tk=tk),
        grid=(b, heads, t // tq),
        in_specs=[
            pl.BlockSpec((1, 1, tq, QK_PAD), lambda i, h, j: (i, h, j, 0)),
            pl.BlockSpec((1, 1, tc, QK_PAD), lambda i, h, j: (i, h, 0, 0)),
            pl.BlockSpec((1, 1, tc, dv), lambda i, h, j: (i, h, 0, 0)),
            pl.BlockSpec((1, 1, t, QK_PAD), lambda i, h, j: (i, h, 0, 0)),
            pl.BlockSpec((1, 1, t, dv), lambda i, h, j: (i, h, 0, 0)),
        ],
        out_specs=pl.BlockSpec((1, tq, dv), lambda i, h, j: (i, j, h)),
        out_shape=jax.ShapeDtypeStruct((b, t, heads * dv), BF16),
        compiler_params=_cparams(("parallel", "parallel", "arbitrary")),
        name="l1_attn",
    )(q, kc, vc, kx, vx)


def _proj_res_kernel(a_ref, w_ref, x_ref, mod_ref, o_ref):
    o_ref[...] = x_ref[...] + mod_ref[0, 2:3, :] * _dot(a_ref[...], w_ref[...])


def _proj_res(a, w, x2, mod, mod_map, tm):
    n, d = x2.shape
    row = lambda t: (t, 0)
    return pl.pallas_call(
        _proj_res_kernel,
        grid=(n // tm,),
        in_specs=[
            pl.BlockSpec((tm, a.shape[1]), row),
            pl.BlockSpec(w.shape, lambda t: (0, 0)),
            pl.BlockSpec((tm, d), row),
            pl.BlockSpec((1, 3, d), mod_map),
        ],
        out_specs=pl.BlockSpec((tm, d), row),
        out_shape=jax.ShapeDtypeStruct((n, d), F32),
        compiler_params=_cparams(("parallel",)),
        name="l1_outproj",
    )(a, w, x2, mod)


def _router_kernel(x_ref, mod_ref, g_ref, rh_ref, rl_ref, o_ref):
    h = _rms_mod(x_ref[...], g_ref[...], mod_ref[0, 0:1, :], mod_ref[0, 1:2, :])
    hh = h.astype(BF16)
    hl = (h - hh.astype(F32)).astype(BF16)
    rh = rh_ref[...]
    logits = _dot(hh, rh) + (_dot(hl, rh) + _dot(hh, rl_ref[...]))
    lane = lax.broadcasted_iota(jnp.int32, logits.shape, 1).astype(F32)
    neg = -jnp.inf
    logits = jnp.where(lane < N_EXPERTS, logits, neg)
    v1 = jnp.max(logits, axis=-1, keepdims=True)
    i1 = jnp.min(jnp.where(logits == v1, lane, float(LANE)), axis=-1, keepdims=True)
    rest = jnp.where(lane == i1, neg, logits)
    v2 = jnp.max(rest, axis=-1, keepdims=True)
    i2 = jnp.min(jnp.where(rest == v2, lane, float(LANE)), axis=-1, keepdims=True)
    e2 = jnp.exp(v2 - v1)
    w1 = 1.0 / (1.0 + e2)
    w2 = e2 / (1.0 + e2)
    o_ref[...] = jnp.where(lane == i1, w1, 0.0) + jnp.where(lane == i2, w2, 0.0)


def _router(x2, mod, mod_map, g, r_hi, r_lo, tm):
    n, d = x2.shape
    row = lambda t: (t, 0)
    const = lambda t: (0, 0)
    return pl.pallas_call(
        _router_kernel,
        grid=(n // tm,),
        in_specs=[
            pl.BlockSpec((tm, d), row),
            pl.BlockSpec((1, 3, d), mod_map),
            pl.BlockSpec((1, d), const),
            pl.BlockSpec(r_hi.shape, const),
            pl.BlockSpec(r_lo.shape, const),
        ],
        out_specs=pl.BlockSpec((tm, LANE), row),
        out_shape=jax.ShapeDtypeStruct((n, LANE), F32),
        compiler_params=_cparams(("parallel",)),
        name="router",
    )(x2, mod, g, r_hi, r_lo)


def _rope_tables(t):
    freqs = QK_ROPE // 4
    rows = t // GRID_W
    row = jnp.repeat(jnp.arange(rows), GRID_W).astype(F32)
    col = jnp.tile(jnp.arange(GRID_W), rows).astype(F32)
    inv_freq = ROPE_BASE ** (-jnp.arange(freqs, dtype=F32) / freqs)
    ang = jnp.stack([row[:, None] * inv_freq, col[:, None] * inv_freq], axis=1)
    cos = jnp.cos(ang)
    sin = jnp.sin(ang)
    c64 = jnp.stack([cos, cos], axis=2).reshape(t, QK_ROPE)
    s64 = jnp.stack([-sin, sin], axis=2).reshape(t, QK_ROPE)
    pad = jnp.zeros((t, LANE - QK_ROPE), F32)
    return jnp.concatenate([c64, pad], axis=1), jnp.concatenate([s64, pad], axis=1)


def _rope_partner_index():
    freqs = QK_ROPE // 4
    d = jnp.arange(QK_ROPE)
    half = (d // freqs) % 2
    return jnp.where(half == 0, d + freqs, d - freqs)


def _mla_weights(o_w_in, o_w_uq, o_w_ukv):
    d = o_w_in.shape[0]
    heads = o_w_ukv.shape[1] // (QK_NOPE + V_HD)
    pidx = _rope_partner_index()
    wq = o_w_in[:, :Q_RANK].astype(BF16)
    wkv = o_w_in[:, Q_RANK:Q_RANK + KV_RANK].astype(BF16)
    pe = o_w_in[:, Q_RANK + KV_RANK:]
    zpad = jnp.zeros((d, LANE - QK_ROPE), F32)
    wpe = jnp.concatenate([pe, zpad, pe[:, pidx], zpad], axis=1).astype(BF16)
    uq = o_w_uq.reshape(Q_RANK, heads, QK_NOPE + QK_ROPE)
    nope, rope = uq[..., :QK_NOPE], uq[..., QK_NOPE:]
    z64 = jnp.zeros((Q_RANK, heads, QK_PAD - QK_NOPE - QK_ROPE), F32)
    wa = jnp.concatenate([nope, rope, z64], axis=-1).reshape(Q_RANK, heads * QK_PAD).astype(BF16)
    wb = jnp.concatenate([rope[..., pidx], jnp.zeros((Q_RANK, heads, LANE - QK_ROPE), F32)],
                         axis=-1).reshape(Q_RANK, heads * LANE).astype(BF16)
    return wq, wkv, wpe, wa, wb, o_w_ukv.astype(BF16)


def kernel(x, c, ctx, c_ctx, w_ada, b_ada, norm_g, e_w_in, e_conv_w, e_conv_b, e_ln_a_g, e_ln_a_b, e_ln_v_g, e_ln_v_b, e_w_s, e_b_s, e_w_out, e_ffn_w1, e_ffn_w3, e_ffn_w2, o_w_in, o_g_q, o_g_kv, o_w_uq, o_w_ukv, o_w_o, o_router, o_exp_w1, o_exp_w3, o_exp_w2, final_g):
    b, t, d = x.shape
    tc = ctx.shape[1]
    n, nc = b * t, b * tc
    depth = w_ada.shape[0]
    assert depth == 2 and t % GRID_W == 0 and t % CHUNK == 0 and tc % CHUNK == 0

    tm_x = min(512, t)
    tm_c = min(256, tc)

    rows = 16
    cc = jnp.concatenate([c, c_ctx[None], jnp.zeros((rows - b - 1, d), F32)], axis=0)
    mods = _ada(cc, w_ada, b_ada).reshape(depth, rows, 6, d)
    xmap = lambda tm: (lambda i: (i // (t // tm), 0, 0))
    cmap = lambda i: (b, 0, 0)

    x2 = x.reshape(n, d)
    c2 = ctx.reshape(nc, d)
    row1 = lambda v: v.reshape(1, -1)

    w_in = e_w_in[0].astype(BF16)
    w_out = e_w_out[0].astype(BF16)
    w_s = e_w_s[0].astype(BF16)
    ch = e_conv_w.shape[-1]
    bias_map = jnp.repeat(e_b_s[0].T, ch // B_HEADS, axis=1)
    conv_w = jnp.concatenate([e_conv_w[0], jnp.zeros((1, ch), F32)], axis=0)
    f1 = e_ffn_w1.astype(BF16)
    f3 = e_ffn_w3.astype(BF16)
    f2 = e_ffn_w2.astype(BF16)
    dff = f1.shape[-1]
    tf = dff // 2 if (dff // 2) % LANE == 0 else dff

    def layer0(s2, seq, tm, mod_a, mod_f, mod_map, tm_f):
        a, u, v = _inproj(s2, mod_a, mod_map(tm), row1(norm_g[0, 0]), w_in,
                          row1(e_ln_v_g[0]), row1(e_ln_v_b[0]), tm)
        s2 = _mixer(a, u, v, s2, mod_a, mod_map(tm), conv_w, row1(e_conv_b[0]),
                    row1(e_ln_a_g[0]), row1(e_ln_a_b[0]), w_s, bias_map, w_out, tm, seq)
        return _ffn(s2, mod_f, mod_map(tm_f), row1(norm_g[0, 1]), f1, f3, f2, tm_f, tf)

    m0a, m0f = mods[0, :, 0:3], mods[0, :, 3:6]
    x2 = layer0(x2, t, tm_x, m0a, m0f, xmap, tm_x)
    c2 = layer0(c2, tc, tm_c, m0a, m0f, lambda tm: cmap, tm_c)

    m1a, m1f = mods[1, :, 0:3], mods[1, :, 3:6]
    wq, wkv, wpe, wa, wb, wukv = _mla_weights(o_w_in[0], o_w_uq[0], o_w_ukv[0])
    cos, sin = _rope_tables(t)
    ones = jnp.concatenate([jnp.ones((tc, QK_ROPE), F32), jnp.zeros((tc, LANE - QK_ROPE), F32)], axis=1)
    zeros = jnp.zeros((tc, LANE), F32)
    g10 = row1(norm_g[1, 0])
    proj = functools.partial(_mla_proj, g=g10, wq=wq, wkv=wkv, wpe=wpe, gq=row1(o_g_q[0]),
                             gkv=row1(o_g_kv[0]), wa=wa, wb=wb, wukv=wukv)
    q_x, k_x, v_x = proj(x2.reshape(b, t, d), m1a, xmap(tm_x), cos=cos, sin=sin, tm=tm_x)
    _, k_c, v_c = proj(c2.reshape(b, tc, d), m1a, cmap, cos=ones, sin=zeros, tm=tm_c)
    att = _attention(q_x, k_c, v_c, k_x, v_x, tq=min(512, t), tk=min(1024, t))
    x2 = _proj_res(att.reshape(n, -1), o_w_o[0].astype(BF16), x2, m1a, xmap(tm_x), tm_x)

    r_pad = jnp.concatenate([o_router[0], jnp.zeros((d, LANE - N_EXPERTS), F32)], axis=1)
    r_hi = r_pad.astype(BF16)
    r_lo = (r_pad - r_hi.astype(F32)).astype(BF16)
    g11 = row1(norm_g[1, 1])
    gates = _router(x2, m1f, xmap(tm_x), g11, r_hi, r_lo, tm_x)
    tm_e = min(1024, t)
    dexp = o_exp_w1.shape[-1]
    out = _ffn(x2, m1f, xmap(tm_e), g11, o_exp_w1[0].astype(BF16), o_exp_w3[0].astype(BF16),
               o_exp_w2[0].astype(BF16), tm_e, 512 if dexp % 512 == 0 else dexp,
               gates=gates, final_g=row1(final_g))
    return out.reshape(b, t, d)
```

```python
import functools

import jax
import jax.numpy as jnp
from jax import lax
from jax.experimental import pallas as pl
from jax.experimental.pallas import tpu as pltpu

F32 = jnp.float32
BF16 = jnp.bfloat16

EPS = 1e-6
GRID_W = 64
CONV_K = 31
CHUNK = 128
B_HEADS = 8
QK_NOPE = 128
QK_ROPE = 64
V_HD = 128
Q_RANK = 384
KV_RANK = 256
ROPE_BASE = 10000.0
N_EXPERTS = 8
ATTN_SCALE = (QK_NOPE + QK_ROPE) ** -0.5
LOG2_E = 1.4426950408889634
Q_SCALE = ATTN_SCALE * LOG2_E

LANE = 128
HALO = 16
CONV_ROWS = 64
QK_PAD = 256
VMEM_LIMIT = 52 * 1024 * 1024


def _cparams(sem):
    return pltpu.CompilerParams(dimension_semantics=sem, vmem_limit_bytes=VMEM_LIMIT)


def _silu(x):
    return x * jax.nn.sigmoid(x)


def _rms_mod(x, g, shift, scale):
    y = x * lax.rsqrt(jnp.mean(x * x, axis=-1, keepdims=True) + EPS) * g
    return y * (1.0 + scale) + shift


def _rms(x, g):
    return x * lax.rsqrt(jnp.mean(x * x, axis=-1, keepdims=True) + EPS) * g


def _dot(a, b):
    return jnp.dot(a, b, preferred_element_type=F32)


def _ada_kernel(cc_ref, w_ref, b_ref, o_ref):
    s = _silu(cc_ref[...]).astype(BF16)
    o_ref[0] = _dot(s, w_ref[0].astype(BF16)) + b_ref[0]


def _ada(cc, w_ada, b_ada):
    depth, d, d6 = w_ada.shape
    rows = cc.shape[0]
    tn = 1536
    return pl.pallas_call(
        _ada_kernel,
        grid=(depth, d6 // tn),
        in_specs=[
            pl.BlockSpec((rows, d), lambda i, j: (0, 0)),
            pl.BlockSpec((1, d, tn), lambda i, j: (i, 0, j)),
            pl.BlockSpec((1, 1, tn), lambda i, j: (i, 0, j)),
        ],
        out_specs=pl.BlockSpec((1, rows, tn), lambda i, j: (i, 0, j)),
        out_shape=jax.ShapeDtypeStruct((depth, rows, d6), F32),
        compiler_params=_cparams(("parallel", "parallel")),
        name="ada",
    )(cc, w_ada, b_ada.reshape(depth, 1, d6))


def _inproj_kernel(x_ref, mod_ref, g_ref, w_ref, lng_ref, lnb_ref, a_ref, u_ref, v_ref):
    ch = a_ref.shape[1]
    h = _rms_mod(x_ref[...], g_ref[...], mod_ref[0, 0:1, :], mod_ref[0, 1:2, :]).astype(BF16)
    z = _dot(h, w_ref[...])
    a = z[:, 0:ch] * jax.nn.sigmoid(z[:, ch:2 * ch])
    u = jax.nn.gelu(z[:, 2 * ch:3 * ch])
    v = jax.nn.gelu(z[:, 3 * ch:4 * ch])
    mu = jnp.mean(v, axis=-1, keepdims=True)
    vc = v - mu
    var = jnp.mean(vc * vc, axis=-1, keepdims=True)
    v = vc * lax.rsqrt(var + EPS) * lng_ref[...] + lnb_ref[...]
    a_ref[...] = a.astype(BF16)
    u_ref[...] = u.astype(BF16)
    v_ref[...] = v.astype(BF16)


def _inproj(x2, mod, mod_map, g, w_in, ln_g, ln_b, tm):
    n, d = x2.shape
    ch = w_in.shape[1] // 4
    row = lambda t: (t, 0)
    const = lambda t: (0, 0)
    out = jax.ShapeDtypeStruct((n, ch), BF16)
    return pl.pallas_call(
        _inproj_kernel,
        grid=(n // tm,),
        in_specs=[
            pl.BlockSpec((tm, d), row),
            pl.BlockSpec((1, 3, d), mod_map),
            pl.BlockSpec((1, d), const),
            pl.BlockSpec(w_in.shape, const),
            pl.BlockSpec((1, ch), const),
            pl.BlockSpec((1, ch), const),
        ],
        out_specs=[pl.BlockSpec((tm, ch), row)] * 3,
        out_shape=[out, out, out],
        compiler_params=_cparams(("parallel",)),
        name="l0_inproj",
    )(x2, mod, g, w_in, ln_g, ln_b)


def _mixer_kernel(a_ref, ap_ref, an_ref, u_ref, v_ref, x_ref, mod_ref, cw_ref, cb_ref,
                  lag_ref, lab_ref, ws_ref, bs_ref, wo_ref, o_ref,
                  ext_ref, sh_ref, cat_ref, *, tiles_per_seq):
    tm, ch = a_ref.shape
    t = pl.program_id(0)
    first = (t % tiles_per_seq) == 0
    last = (t % tiles_per_seq) == tiles_per_seq - 1

    ext_ref[0:HALO, :] = jnp.where(first, 0.0, ap_ref[...].astype(F32))
    ext_ref[HALO:HALO + tm, :] = a_ref[...].astype(F32)
    ext_ref[HALO + tm:, :] = jnp.where(last, 0.0, an_ref[...].astype(F32))
    span = tm + 24
    for r in range(1, 8):
        sh_ref[r - 1] = ext_ref[r:r + span, :]

    off0 = HALO - CONV_K // 2
    cb = cb_ref[...]
    lag = lag_ref[...]
    lab = lab_ref[...]

    def conv_block(rb, carry):
        base = pl.multiple_of(rb * CONV_ROWS, CONV_ROWS)
        acc = jnp.broadcast_to(cb, (CONV_ROWS, ch))
        for k in range(CONV_K):
            q, r = divmod(off0 + k, 8)
            start = pl.multiple_of(base + 8 * q, 8)
            if r == 0:
                win = ext_ref[pl.ds(start, CONV_ROWS), :]
            else:
                win = sh_ref[r - 1, pl.ds(start, CONV_ROWS), :]
            acc = acc + cw_ref[k:k + 1, :] * win
        mu = jnp.mean(acc, axis=-1, keepdims=True)
        c = acc - mu
        var = jnp.mean(c * c, axis=-1, keepdims=True)
        y = c * lax.rsqrt(var + EPS) * lag + lab
        cat_ref[pl.ds(base, CONV_ROWS), 0:ch] = _silu(y).astype(BF16)
        return carry

    lax.fori_loop(0, tm // CONV_ROWS, conv_block, 0)

    lane = lax.broadcasted_iota(jnp.int32, (CHUNK, LANE), 1)
    lo_half = lane < (LANE // 2)
    for c in range(tm // CHUNK):
        rows = slice(c * CHUNK, (c + 1) * CHUNK)
        for p in range(ch // LANE):
            cols = slice(p * LANE, (p + 1) * LANE)
            vv = v_ref[rows, cols]
            s0 = _dot(ws_ref[2 * p], vv)
            s1 = _dot(ws_ref[2 * p + 1], vv)
            sv = jnp.where(lo_half, s0, s1) + bs_ref[:, cols]
            bo = u_ref[rows, cols].astype(F32) * sv
            cat_ref[rows, ch + p * LANE:ch + (p + 1) * LANE] = bo.astype(BF16)

    y = _dot(cat_ref[...], wo_ref[...])
    o_ref[...] = x_ref[...] + mod_ref[0, 2:3, :] * y


def _mixer(a, u, v, x2, mod, mod_map, conv_w, conv_b, la_g, la_b, w_s, bias_map, w_out, tm, seq):
    n, d = x2.shape
    ch = a.shape[1]
    hb = tm // HALO
    nhb = n // HALO
    row = lambda t: (t, 0)
    const = lambda t: (0, 0)
    kern = functools.partial(_mixer_kernel, tiles_per_seq=seq // tm)
    return pl.pallas_call(
        kern,
        grid=(n // tm,),
        in_specs=[
            pl.BlockSpec((tm, ch), row),
            pl.BlockSpec((HALO, ch), lambda t: (jnp.maximum(t * hb - 1, 0), 0)),
            pl.BlockSpec((HALO, ch), lambda t: (jnp.minimum((t + 1) * hb, nhb - 1), 0)),
            pl.BlockSpec((tm, ch), row),
            pl.BlockSpec((tm, ch), row),
            pl.BlockSpec((tm, d), row),
            pl.BlockSpec((1, 3, d), mod_map),
            pl.BlockSpec(conv_w.shape, const),
            pl.BlockSpec((1, ch), const),
            pl.BlockSpec((1, ch), const),
            pl.BlockSpec((1, ch), const),
            pl.BlockSpec(w_s.shape, lambda t: (0, 0, 0)),
            pl.BlockSpec(bias_map.shape, const),
            pl.BlockSpec(w_out.shape, const),
        ],
        out_specs=pl.BlockSpec((tm, d), row),
        out_shape=jax.ShapeDtypeStruct((n, d), F32),
        scratch_shapes=[
            pltpu.VMEM((tm + 2 * HALO, ch), F32),
            pltpu.VMEM((7, tm + 24, ch), F32),
            pltpu.VMEM((tm, 2 * ch), BF16),
        ],
        compiler_params=_cparams(("parallel",)),
        name="l0_mixer",
    )(a, a, a, u, v, x2, mod, conv_w, conv_b, la_g, la_b, w_s, bias_map, w_out)


def _ffn_kernel(x_ref, mod_ref, g_ref, w1_ref, w3_ref, w2_ref, o_ref, h_ref, acc_ref):
    f = pl.program_id(1)

    @pl.when(f == 0)
    def _():
        h = _rms_mod(x_ref[...], g_ref[...], mod_ref[0, 0:1, :], mod_ref[0, 1:2, :])
        h_ref[...] = h.astype(BF16)
        acc_ref[...] = jnp.zeros_like(acc_ref)

    h = h_ref[...]
    act = _silu(_dot(h, w1_ref[...])) * _dot(h, w3_ref[...])
    acc_ref[...] += _dot(act.astype(BF16), w2_ref[...])

    @pl.when(f == pl.num_programs(1) - 1)
    def _():
        o_ref[...] = x_ref[...] + mod_ref[0, 2:3, :] * acc_ref[...]


def _ffn(x2, mod, mod_map, g, w1, w3, w2, tm, tf):
    n, d = x2.shape
    dff = w1.shape[1]
    row = lambda t, f: (t, 0)
    return pl.pallas_call(
        _ffn_kernel,
        grid=(n // tm, dff // tf),
        in_specs=[
            pl.BlockSpec((tm, d), row),
            pl.BlockSpec((1, 3, d), lambda t, f: mod_map(t)),
            pl.BlockSpec((1, d), lambda t, f: (0, 0)),
            pl.BlockSpec((d, tf), lambda t, f: (0, f)),
            pl.BlockSpec((d, tf), lambda t, f: (0, f)),
            pl.BlockSpec((tf, d), lambda t, f: (f, 0)),
        ],
        out_specs=pl.BlockSpec((tm, d), row),
        out_shape=jax.ShapeDtypeStruct((n, d), F32),
        scratch_shapes=[pltpu.VMEM((tm, d), BF16), pltpu.VMEM((tm, d), F32)],
        compiler_params=_cparams(("parallel", "arbitrary")),
        name="ffn",
    )(x2, mod, g, w1, w3, w2)


def _mla_proj_kernel(x_ref, mod_ref, g_ref, wq_ref, wkv_ref, wpe_ref, gq_ref, gkv_ref,
                     wa_ref, wb_ref, wukv_ref, cos_ref, sin_ref,
                     q_ref, k_ref, v_ref):
    heads = q_ref.shape[1]
    h = _rms_mod(x_ref[...], g_ref[...], mod_ref[0, 0:1, :], mod_ref[0, 1:2, :]).astype(BF16)
    cos = cos_ref[...]
    sin = sin_ref[...]
    zpe = _dot(h, wpe_ref[...])
    kr = (zpe[:, 0:LANE] * cos + zpe[:, LANE:2 * LANE] * sin).astype(BF16)

    qn = _rms(_dot(h, wq_ref[...]), gq_ref[...]).astype(BF16)
    qa = _dot(qn, wa_ref[...])
    qb = _dot(qn, wb_ref[...])
    kvn = _rms(_dot(h, wkv_ref[...]), gkv_ref[...]).astype(BF16)
    kv = _dot(kvn, wukv_ref[...])
    for hd in range(heads):
        q_ref[0, hd, :, 0:LANE] = (qa[:, hd * QK_PAD:hd * QK_PAD + LANE] * Q_SCALE).astype(BF16)
        qr = (qa[:, hd * QK_PAD + LANE:(hd + 1) * QK_PAD] * cos
              + qb[:, hd * LANE:(hd + 1) * LANE] * sin)
        q_ref[0, hd, :, LANE:QK_PAD] = (qr * Q_SCALE).astype(BF16)
        k_ref[0, hd, :, 0:LANE] = kv[:, hd * 2 * LANE:hd * 2 * LANE + LANE].astype(BF16)
        k_ref[0, hd, :, LANE:QK_PAD] = kr
        v_ref[0, hd, :, :] = kv[:, hd * 2 * LANE + LANE:(hd + 1) * 2 * LANE].astype(BF16)


def _mla_proj(x3, mod, mod_map, g, wq, wkv, wpe, gq, gkv, wa, wb, wukv, cos, sin, tm):
    b, t, d = x3.shape
    heads = wukv.shape[1] // (2 * LANE)
    tps = t // tm
    row = lambda i: (i // tps, i % tps, 0)
    const = lambda i: (0, 0)
    tab = lambda i: (i % tps, 0)
    hrow = lambda i: (i // tps, 0, i % tps, 0)
    return pl.pallas_call(
        _mla_proj_kernel,
        grid=(b * tps,),
        in_specs=[
            pl.BlockSpec((None, tm, d), row),
            pl.BlockSpec((1, 3, d), mod_map),
            pl.BlockSpec((1, d), const),
            pl.BlockSpec(wq.shape, const),
            pl.BlockSpec(wkv.shape, const),
            pl.BlockSpec(wpe.shape, const),
            pl.BlockSpec(gq.shape, const),
            pl.BlockSpec(gkv.shape, const),
            pl.BlockSpec(wa.shape, const),
            pl.BlockSpec(wb.shape, const),
            pl.BlockSpec(wukv.shape, const),
            pl.BlockSpec((tm, LANE), tab),
            pl.BlockSpec((tm, LANE), tab),
        ],
        out_specs=[
            pl.BlockSpec((1, heads, tm, QK_PAD), hrow),
            pl.BlockSpec((1, heads, tm, QK_PAD), hrow),
            pl.BlockSpec((1, heads, tm, LANE), hrow),
        ],
        out_shape=[
            jax.ShapeDtypeStruct((b, heads, t, QK_PAD), BF16),
            jax.ShapeDtypeStruct((b, heads, t, QK_PAD), BF16),
            jax.ShapeDtypeStruct((b, heads, t, LANE), BF16),
        ],
        compiler_params=_cparams(("parallel",)),
        name="l1_proj",
    )(x3, mod, g, wq, wkv, wpe, gq, gkv, wa, wb, wukv, cos, sin)


def _attn_kernel(q_ref, kc_ref, vc_ref, kx_ref, vx_ref, o_ref, *, tk):
    q = q_ref[0, 0]
    tq = q.shape[0]

    def step(k, v, m, l, acc):
        s = lax.dot_general(q, k, (((1,), (1,)), ((), ())), preferred_element_type=F32)
        m_new = jnp.maximum(m, jnp.max(s, axis=-1, keepdims=True))
        alpha = jnp.exp2(m - m_new)
        p = jnp.exp2(s - m_new)
        l = alpha * l + jnp.sum(p, axis=-1, keepdims=True)
        acc = alpha * acc + _dot(p.astype(BF16), v)
        return m_new, l, acc

    m = jnp.full((tq, 1), -jnp.inf, F32)
    l = jnp.zeros((tq, 1), F32)
    acc = jnp.zeros((tq, vc_ref.shape[-1]), F32)
    m, l, acc = step(kc_ref[0, 0], vc_ref[0, 0], m, l, acc)
    for c in range(kx_ref.shape[2] // tk):
        m, l, acc = step(kx_ref[0, 0, c * tk:(c + 1) * tk, :], vx_ref[0, 0, c * tk:(c + 1) * tk, :],
                         m, l, acc)
    o_ref[0] = (acc / l).astype(BF16)


def _attention(q, kc, vc, kx, vx, tq, tk):
    b, heads, t, _ = q.shape
    tc = kc.shape[2]
    dv = vx.shape[-1]
    return pl.pallas_call(
        functools.partial(_attn_kernel, tk=tk),
        grid=(b, heads, t // tq),
        in_specs=[
            pl.BlockSpec((1, 1, tq, QK_PAD), lambda i, h, j: (i, h, j, 0)),
            pl.BlockSpec((1, 1, tc, QK_PAD), lambda i, h, j: (i, h, 0, 0)),
            pl.BlockSpec((1, 1, tc, dv), lambda i, h, j: (i, h, 0, 0)),
            pl.BlockSpec((1, 1, t, QK_PAD), lambda i, h, j: (i, h, 0, 0)),
            pl.BlockSpec((1, 1, t, dv), lambda i, h, j: (i, h, 0, 0)),
        ],
        out_specs=pl.BlockSpec((1, tq, dv), lambda i, h, j: (i, j, h)),
        out_shape=jax.ShapeDtypeStruct((b, t, heads * dv), BF16),
        compiler_params=_cparams(("parallel", "parallel", "arbitrary")),
        name="l1_attn",
    )(q, kc, vc, kx, vx)


def _outproj_router_kernel(a_ref, w_ref, x_ref, moda_ref, modf_ref, g_ref, rh_ref, rl_ref,
                           xo_ref, h_ref, sel_ref):
    x = x_ref[...] + moda_ref[0, 2:3, :] * _dot(a_ref[...], w_ref[...])
    xo_ref[...] = x
    h = _rms_mod(x, g_ref[...], modf_ref[0, 0:1, :], modf_ref[0, 1:2, :])
    h_ref[...] = h
    hh = h.astype(BF16)
    hl = (h - hh.astype(F32)).astype(BF16)
    rh = rh_ref[...]
    logits = _dot(hh, rh) + (_dot(hl, rh) + _dot(hh, rl_ref[...]))
    lane = lax.broadcasted_iota(jnp.int32, logits.shape, 1).astype(F32)
    neg = -jnp.inf
    logits = jnp.where(lane < N_EXPERTS, logits, neg)
    v1 = jnp.max(logits, axis=-1, keepdims=True)
    i1 = jnp.min(jnp.where(logits == v1, lane, float(LANE)), axis=-1, keepdims=True)
    rest = jnp.where(lane == i1, neg, logits)
    v2 = jnp.max(rest, axis=-1, keepdims=True)
    i2 = jnp.min(jnp.where(rest == v2, lane, float(LANE)), axis=-1, keepdims=True)
    e2 = jnp.exp(v2 - v1)
    w1 = 1.0 / (1.0 + e2)
    w2 = e2 / (1.0 + e2)
    sel_ref[...] = (jnp.where(lane == 0.0, i1, 0.0) + jnp.where(lane == 1.0, i2, 0.0)
                    + jnp.where(lane == 2.0, w1, 0.0) + jnp.where(lane == 3.0, w2, 0.0))


def _outproj_router(a, w, x2, mod_a, mod_f, mod_map, g, r_hi, r_lo, tm):
    n, d = x2.shape
    row = lambda t: (t, 0)
    const = lambda t: (0, 0)
    return pl.pallas_call(
        _outproj_router_kernel,
        grid=(n // tm,),
        in_specs=[
            pl.BlockSpec((tm, a.shape[1]), row),
            pl.BlockSpec(w.shape, const),
            pl.BlockSpec((tm, d), row),
            pl.BlockSpec((1, 3, d), mod_map),
            pl.BlockSpec((1, 3, d), mod_map),
            pl.BlockSpec((1, d), const),
            pl.BlockSpec(r_hi.shape, const),
            pl.BlockSpec(r_lo.shape, const),
        ],
        out_specs=[pl.BlockSpec((tm, d), row), pl.BlockSpec((tm, d), row), pl.BlockSpec((tm, LANE), row)],
        out_shape=[jax.ShapeDtypeStruct((n, d), F32), jax.ShapeDtypeStruct((n, d), F32),
                   jax.ShapeDtypeStruct((n, LANE), F32)],
        compiler_params=_cparams(("parallel",)),
        name="l1_outproj_router",
    )(a, w, x2, mod_a, mod_f, g, r_hi, r_lo)


def _routing_tables(sel, tm):
    n = sel.shape[0]
    ids = sel[:, 0:2].astype(jnp.int32).reshape(2 * n)
    onehot = (ids[:, None] == jnp.arange(N_EXPERTS, dtype=jnp.int32)[None, :]).astype(jnp.int32)
    csum = jnp.cumsum(onehot, axis=0)
    rank = jnp.sum(csum * onehot, axis=1) - 1
    tiles = (csum[-1] + tm - 1) // tm
    tile_end = jnp.cumsum(tiles)
    start = (tile_end - tiles) * tm
    pos = start[ids] + rank
    n_tiles = (2 * n) // tm + N_EXPERTS
    tile_expert = jnp.sum(jnp.arange(n_tiles, dtype=jnp.int32)[:, None] >= tile_end[None, :], axis=1)
    tile_expert = jnp.minimum(tile_expert, N_EXPERTS - 1).astype(jnp.int32)
    return pos.astype(jnp.int32), tile_expert, tile_end[-1:].astype(jnp.int32), n_tiles


def _row_copy(src, src_row, dst, dst_row, sem):
    return pltpu.make_async_copy(src.at[pl.ds(src_row, 1)], dst.at[pl.ds(dst_row, 1)], sem)


def _scatter_kernel(pos_ref, h_ref, init_ref, hs_ref, sem):
    del init_ref
    tm = h_ref.shape[0]

    def issue(r, carry):
        _row_copy(h_ref, r, hs_ref, pos_ref[0, 2 * r], sem).start()
        _row_copy(h_ref, r, hs_ref, pos_ref[0, 2 * r + 1], sem).start()
        return carry

    lax.fori_loop(0, tm, issue, 0, unroll=8)
    for _ in range(2):
        pltpu.make_async_copy(h_ref, hs_ref.at[pl.ds(0, tm)], sem).wait()


def _scatter_rows(pos2, hp, hs_init, tm):
    n, w = hp.shape
    return pl.pallas_call(
        _scatter_kernel,
        grid=(n // tm,),
        in_specs=[
            pl.BlockSpec((None, 1, 2 * tm), lambda t: (t, 0, 0), memory_space=pltpu.SMEM),
            pl.BlockSpec((tm, w), lambda t: (t, 0)),
            pl.BlockSpec(memory_space=pl.ANY),
        ],
        out_specs=pl.BlockSpec(memory_space=pl.ANY),
        out_shape=jax.ShapeDtypeStruct(hs_init.shape, hs_init.dtype),
        scratch_shapes=[pltpu.SemaphoreType.DMA(())],
        input_output_aliases={2: 0},
        compiler_params=_cparams(("arbitrary",)),
        name="moe_scatter",
    )(pos2, hp, hs_init)


def _moe_ffn_kernel(te_ref, nt_ref, hs_ref, w1_ref, w3_ref, w2_ref, y_ref, h_ref, acc_ref):
    del te_ref
    j = pl.program_id(0)
    f = pl.program_id(1)
    last = f == pl.num_programs(1) - 1
    valid = j < nt_ref[0]

    @pl.when(valid & (f == 0))
    def _():
        h_ref[...] = hs_ref[...].astype(BF16)
        acc_ref[...] = jnp.zeros_like(acc_ref)

    @pl.when(valid)
    def _():
        h = h_ref[...]
        act = _silu(_dot(h, w1_ref[0])) * _dot(h, w3_ref[0])
        acc_ref[...] += _dot(act.astype(BF16), w2_ref[0])

    @pl.when(valid & last)
    def _():
        y_ref[...] = acc_ref[...]

    @pl.when(jnp.logical_not(valid) & last)
    def _():
        y_ref[...] = jnp.zeros_like(y_ref)


def _moe_ffn(tile_expert, n_tiles_used, hs, w1, w3, w2, tm, tf):
    p, w = hs.shape
    _, d, dff = w1.shape
    grid_spec = pltpu.PrefetchScalarGridSpec(
        num_scalar_prefetch=2,
        grid=(p // tm, dff // tf),
        in_specs=[
            pl.BlockSpec((tm, w), lambda j, f, te, nt: (j, 0)),
            pl.BlockSpec((1, d, tf), lambda j, f, te, nt: (te[j], 0, f)),
            pl.BlockSpec((1, d, tf), lambda j, f, te, nt: (te[j], 0, f)),
            pl.BlockSpec((1, tf, d), lambda j, f, te, nt: (te[j], f, 0)),
        ],
        out_specs=pl.BlockSpec((tm, w), lambda j, f, te, nt: (j, 0)),
        scratch_shapes=[pltpu.VMEM((tm, d), BF16), pltpu.VMEM((tm, d), F32)],
    )
    return pl.pallas_call(
        _moe_ffn_kernel,
        grid_spec=grid_spec,
        out_shape=jax.ShapeDtypeStruct((p, w), F32),
        compiler_params=_cparams(("parallel", "arbitrary")),
        name="moe_ffn",
    )(tile_expert, n_tiles_used, hs, w1, w3, w2)


def _combine_kernel(pos_ref, x_ref, mod_ref, sel_ref, fg_ref, ys_ref, o_ref, buf0, buf1, sem):
    tm = x_ref.shape[0]

    def issue(r, carry):
        _row_copy(ys_ref, pos_ref[0, 2 * r], buf0, r, sem).start()
        _row_copy(ys_ref, pos_ref[0, 2 * r + 1], buf1, r, sem).start()
        return carry

    lax.fori_loop(0, tm, issue, 0, unroll=8)
    for buf in (buf0, buf1):
        pltpu.make_async_copy(ys_ref.at[pl.ds(0, tm)], buf, sem).wait()
    sel = sel_ref[...]
    y = sel[:, 2:3] * buf0[...] + sel[:, 3:4] * buf1[...]
    o_ref[...] = _rms(x_ref[...] + mod_ref[0, 2:3, :] * y, fg_ref[...])


def _combine(pos2, x2, mod, mod_map, sel, final_g, ys, tm):
    n, d = x2.shape
    w = ys.shape[1]
    row = lambda t: (t, 0)
    return pl.pallas_call(
        _combine_kernel,
        grid=(n // tm,),
        in_specs=[
            pl.BlockSpec((None, 1, 2 * tm), lambda t: (t, 0, 0), memory_space=pltpu.SMEM),
            pl.BlockSpec((tm, d), row),
            pl.BlockSpec((1, 3, d), mod_map),
            pl.BlockSpec((tm, LANE), row),
            pl.BlockSpec((1, d), lambda t: (0, 0)),
            pl.BlockSpec(memory_space=pl.ANY),
        ],
        out_specs=pl.BlockSpec((tm, d), row),
        out_shape=jax.ShapeDtypeStruct((n, d), F32),
        scratch_shapes=[pltpu.VMEM((tm, w), F32), pltpu.VMEM((tm, w), F32),
                        pltpu.SemaphoreType.DMA(())],
        compiler_params=_cparams(("arbitrary",)),
        name="moe_combine",
    )(pos2, x2, mod, sel, final_g, ys)


def _rope_tables(t):
    freqs = QK_ROPE // 4
    rows = t // GRID_W
    row = jnp.repeat(jnp.arange(rows), GRID_W).astype(F32)
    col = jnp.tile(jnp.arange(GRID_W), rows).astype(F32)
    inv_freq = ROPE_BASE ** (-jnp.arange(freqs, dtype=F32) / freqs)
    ang = jnp.stack([row[:, None] * inv_freq, col[:, None] * inv_freq], axis=1)
    cos = jnp.cos(ang)
    sin = jnp.sin(ang)
    c64 = jnp.stack([cos, cos], axis=2).reshape(t, QK_ROPE)
    s64 = jnp.stack([-sin, sin], axis=2).reshape(t, QK_ROPE)
    pad = jnp.zeros((t, LANE - QK_ROPE), F32)
    return jnp.concatenate([c64, pad], axis=1), jnp.concatenate([s64, pad], axis=1)


def _rope_partner_index():
    freqs = QK_ROPE // 4
    d = jnp.arange(QK_ROPE)
    half = (d // freqs) % 2
    return jnp.where(half == 0, d + freqs, d - freqs)


def _mla_weights(o_w_in, o_w_uq, o_w_ukv):
    d = o_w_in.shape[0]
    heads = o_w_ukv.shape[1] // (QK_NOPE + V_HD)
    pidx = _rope_partner_index()
    wq = o_w_in[:, :Q_RANK].astype(BF16)
    wkv = o_w_in[:, Q_RANK:Q_RANK + KV_RANK].astype(BF16)
    pe = o_w_in[:, Q_RANK + KV_RANK:]
    zpad = jnp.zeros((d, LANE - QK_ROPE), F32)
    wpe = jnp.concatenate([pe, zpad, pe[:, pidx], zpad], axis=1).astype(BF16)
    uq = o_w_uq.reshape(Q_RANK, heads, QK_NOPE + QK_ROPE)
    nope, rope = uq[..., :QK_NOPE], uq[..., QK_NOPE:]
    z64 = jnp.zeros((Q_RANK, heads, QK_PAD - QK_NOPE - QK_ROPE), F32)
    wa = jnp.concatenate([nope, rope, z64], axis=-1).reshape(Q_RANK, heads * QK_PAD).astype(BF16)
    wb = jnp.concatenate([rope[..., pidx], jnp.zeros((Q_RANK, heads, LANE - QK_ROPE), F32)],
                         axis=-1).reshape(Q_RANK, heads * LANE).astype(BF16)
    return wq, wkv, wpe, wa, wb, o_w_ukv.astype(BF16)


def kernel(x, c, ctx, c_ctx, w_ada, b_ada, norm_g, e_w_in, e_conv_w, e_conv_b, e_ln_a_g, e_ln_a_b, e_ln_v_g, e_ln_v_b, e_w_s, e_b_s, e_w_out, e_ffn_w1, e_ffn_w3, e_ffn_w2, o_w_in, o_g_q, o_g_kv, o_w_uq, o_w_ukv, o_w_o, o_router, o_exp_w1, o_exp_w3, o_exp_w2, final_g):
    b, t, d = x.shape
    tc = ctx.shape[1]
    n, nc = b * t, b * tc
    depth = w_ada.shape[0]
    assert depth == 2 and t % GRID_W == 0 and t % CHUNK == 0 and tc % CHUNK == 0

    tm_x = min(512, t)
    tm_c = min(256, tc)

    rows = 16
    cc = jnp.concatenate([c, c_ctx[None], jnp.zeros((rows - b - 1, d), F32)], axis=0)
    mods = _ada(cc, w_ada, b_ada).reshape(depth, rows, 6, d)
    xmap = lambda tm: (lambda i: (i // (t // tm), 0, 0))
    cmap = lambda i: (b, 0, 0)

    x2 = x.reshape(n, d)
    c2 = ctx.reshape(nc, d)
    row1 = lambda v: v.reshape(1, -1)

    w_in = e_w_in[0].astype(BF16)
    w_out = e_w_out[0].astype(BF16)
    w_s = e_w_s[0].astype(BF16)
    ch = e_conv_w.shape[-1]
    bias_map = jnp.repeat(e_b_s[0].T, ch // B_HEADS, axis=1)
    conv_w = jnp.concatenate([e_conv_w[0], jnp.zeros((1, ch), F32)], axis=0)
    f1 = e_ffn_w1[0].astype(BF16)
    f3 = e_ffn_w3[0].astype(BF16)
    f2 = e_ffn_w2[0].astype(BF16)
    dff = f1.shape[-1]
    tf = dff // 2 if (dff // 2) % LANE == 0 else dff

    def layer0(s2, seq, tm, mod_a, mod_f, mod_map, tm_f):
        a, u, v = _inproj(s2, mod_a, mod_map(tm), row1(norm_g[0, 0]), w_in,
                          row1(e_ln_v_g[0]), row1(e_ln_v_b[0]), tm)
        s2 = _mixer(a, u, v, s2, mod_a, mod_map(tm), conv_w, row1(e_conv_b[0]),
                    row1(e_ln_a_g[0]), row1(e_ln_a_b[0]), w_s, bias_map, w_out, tm, seq)
        return _ffn(s2, mod_f, mod_map(tm_f), row1(norm_g[0, 1]), f1, f3, f2, tm_f, tf)

    m0a, m0f = mods[0, :, 0:3], mods[0, :, 3:6]
    x2 = layer0(x2, t, tm_x, m0a, m0f, xmap, tm_x)
    c2 = layer0(c2, tc, tm_c, m0a, m0f, lambda tm: cmap, tm_c)

    m1a, m1f = mods[1, :, 0:3], mods[1, :, 3:6]
    wq, wkv, wpe, wa, wb, wukv = _mla_weights(o_w_in[0], o_w_uq[0], o_w_ukv[0])
    cos, sin = _rope_tables(t)
    ones = jnp.concatenate([jnp.ones((tc, QK_ROPE), F32), jnp.zeros((tc, LANE - QK_ROPE), F32)], axis=1)
    zeros = jnp.zeros((tc, LANE), F32)
    g10 = row1(norm_g[1, 0])
    proj = functools.partial(_mla_proj, g=g10, wq=wq, wkv=wkv, wpe=wpe, gq=row1(o_g_q[0]),
                             gkv=row1(o_g_kv[0]), wa=wa, wb=wb, wukv=wukv)
    q_x, k_x, v_x = proj(x2.reshape(b, t, d), m1a, xmap(tm_x), cos=cos, sin=sin, tm=tm_x)
    _, k_c, v_c = proj(c2.reshape(b, tc, d), m1a, cmap, cos=ones, sin=zeros, tm=tm_c)
    att = _attention(q_x, k_c, v_c, k_x, v_x, tq=min(1024, t), tk=min(512, t))

    r_pad = jnp.concatenate([o_router[0], jnp.zeros((d, LANE - N_EXPERTS), F32)], axis=1)
    r_hi = r_pad.astype(BF16)
    r_lo = (r_pad - r_hi.astype(F32)).astype(BF16)
    x2, hp, sel = _outproj_router(att.reshape(n, -1), o_w_o[0].astype(BF16), x2, m1a, m1f, xmap(tm_x),
                                  row1(norm_g[1, 1]), r_hi, r_lo, tm_x)
    tm_e = 512
    pos, tile_expert, n_used, n_tiles = _routing_tables(sel, tm_e)
    hs = _scatter_rows(pos.reshape(n // tm_x, 1, 2 * tm_x), hp,
                       jnp.zeros((n_tiles * tm_e, d), F32), tm_x)
    dexp = o_exp_w1.shape[-1]
    tf_e = dexp // 2 if (dexp // 2) % LANE == 0 else dexp
    ys = _moe_ffn(tile_expert, n_used, hs, o_exp_w1[0].astype(BF16), o_exp_w3[0].astype(BF16),
                  o_exp_w2[0].astype(BF16), tm_e, tf_e)
    tm_k = min(256, t)
    out = _combine(pos.reshape(n // tm_k, 1, 2 * tm_k), x2, m1f, xmap(tm_k), sel, row1(final_g), ys, tm_k)
    return out.reshape(b, t, d)
```

```python
import functools

import jax
import jax.numpy as jnp
from jax import lax
from jax.experimental import pallas as pl
from jax.experimental.pallas import tpu as pltpu

F32 = jnp.float32
BF16 = jnp.bfloat16

EPS = 1e-6
GRID_W = 64
CONV_K = 31
CHUNK = 128
B_HEADS = 8
QK_NOPE = 128
QK_ROPE = 64
V_HD = 128
Q_RANK = 384
KV_RANK = 256
ROPE_BASE = 10000.0
N_EXPERTS = 8
ATTN_SCALE = (QK_NOPE + QK_ROPE) ** -0.5
LOG2_E = 1.4426950408889634
Q_SCALE = ATTN_SCALE * LOG2_E

LANE = 128
HALO = 16
CONV_ROWS = 64
QK_PAD = 256
ROW_SPLIT = 2
VMEM_LIMIT = 52 * 1024 * 1024


def _cparams(sem):
    return pltpu.CompilerParams(dimension_semantics=sem, vmem_limit_bytes=VMEM_LIMIT)


def _silu(x):
    return x * jax.nn.sigmoid(x)


def _rms_mod(x, g, shift, scale):
    y = x * lax.rsqrt(jnp.mean(x * x, axis=-1, keepdims=True) + EPS) * g
    return y * (1.0 + scale) + shift


def _rms(x, g):
    return x * lax.rsqrt(jnp.mean(x * x, axis=-1, keepdims=True) + EPS) * g


def _dot(a, b):
    return jnp.dot(a, b, preferred_element_type=F32)


def _ada_kernel(cc_ref, w_ref, b_ref, o_ref):
    s = _silu(cc_ref[...]).astype(BF16)
    o_ref[0] = _dot(s, w_ref[0].astype(BF16)) + b_ref[0]


def _ada(cc, w_ada, b_ada):
    depth, d, d6 = w_ada.shape
    rows = cc.shape[0]
    tn = 1536
    return pl.pallas_call(
        _ada_kernel,
        grid=(depth, d6 // tn),
        in_specs=[
            pl.BlockSpec((rows, d), lambda i, j: (0, 0)),
            pl.BlockSpec((1, d, tn), lambda i, j: (i, 0, j)),
            pl.BlockSpec((1, 1, tn), lambda i, j: (i, 0, j)),
        ],
        out_specs=pl.BlockSpec((1, rows, tn), lambda i, j: (i, 0, j)),
        out_shape=jax.ShapeDtypeStruct((depth, rows, d6), F32),
        compiler_params=_cparams(("parallel", "parallel")),
        name="ada",
    )(cc, w_ada, b_ada.reshape(depth, 1, d6))


def _inproj_kernel(x_ref, mod_ref, g_ref, w_ref, lng_ref, lnb_ref, a_ref, u_ref, v_ref):
    ch = a_ref.shape[1]
    h = _rms_mod(x_ref[...], g_ref[...], mod_ref[0, 0:1, :], mod_ref[0, 1:2, :]).astype(BF16)
    z = _dot(h, w_ref[...])
    a = z[:, 0:ch] * jax.nn.sigmoid(z[:, ch:2 * ch])
    u = jax.nn.gelu(z[:, 2 * ch:3 * ch])
    v = jax.nn.gelu(z[:, 3 * ch:4 * ch])
    mu = jnp.mean(v, axis=-1, keepdims=True)
    vc = v - mu
    var = jnp.mean(vc * vc, axis=-1, keepdims=True)
    v = vc * lax.rsqrt(var + EPS) * lng_ref[...] + lnb_ref[...]
    a_ref[...] = a.astype(BF16)
    u_ref[...] = u.astype(BF16)
    v_ref[...] = v.astype(BF16)


def _inproj(x2, mod, mod_map, g, w_in, ln_g, ln_b, tm):
    n, d = x2.shape
    ch = w_in.shape[1] // 4
    row = lambda t: (t, 0)
    const = lambda t: (0, 0)
    out = jax.ShapeDtypeStruct((n, ch), BF16)
    return pl.pallas_call(
        _inproj_kernel,
        grid=(n // tm,),
        in_specs=[
            pl.BlockSpec((tm, d), row),
            pl.BlockSpec((1, 3, d), mod_map),
            pl.BlockSpec((1, d), const),
            pl.BlockSpec(w_in.shape, const),
            pl.BlockSpec((1, ch), const),
            pl.BlockSpec((1, ch), const),
        ],
        out_specs=[pl.BlockSpec((tm, ch), row)] * 3,
        out_shape=[out, out, out],
        compiler_params=_cparams(("parallel",)),
        name="l0_inproj",
    )(x2, mod, g, w_in, ln_g, ln_b)


def _mixer_kernel(a_ref, ap_ref, an_ref, u_ref, v_ref, x_ref, mod_ref, cw_ref, cb_ref,
                  lag_ref, lab_ref, ws_ref, bs_ref, wo_ref, o_ref,
                  ext_ref, sh_ref, conv_ref, cat_ref, *, tiles_per_seq):
    tm, ch = a_ref.shape
    t = pl.program_id(0)
    first = (t % tiles_per_seq) == 0
    last = (t % tiles_per_seq) == tiles_per_seq - 1

    ext_ref[0:HALO, :] = jnp.where(first, 0.0, ap_ref[...].astype(F32))
    ext_ref[HALO:HALO + tm, :] = a_ref[...].astype(F32)
    ext_ref[HALO + tm:, :] = jnp.where(last, 0.0, an_ref[...].astype(F32))
    span = tm + 24
    for r in range(1, 8):
        sh_ref[r - 1] = ext_ref[r:r + span, :]

    off0 = HALO - CONV_K // 2
    cb = cb_ref[...]

    def conv_block(rb, carry):
        base = pl.multiple_of(rb * CONV_ROWS, CONV_ROWS)
        acc = jnp.broadcast_to(cb, (CONV_ROWS, ch))
        for k in range(CONV_K):
            q, r = divmod(off0 + k, 8)
            start = pl.multiple_of(base + 8 * q, 8)
            if r == 0:
                win = ext_ref[pl.ds(start, CONV_ROWS), :]
            else:
                win = sh_ref[r - 1, pl.ds(start, CONV_ROWS), :]
            acc = acc + cw_ref[k:k + 1, :] * win
        conv_ref[pl.ds(base, CONV_ROWS), :] = acc
        return carry

    lax.fori_loop(0, tm // CONV_ROWS, conv_block, 0)

    acc = conv_ref[...]
    mu = jnp.mean(acc, axis=-1, keepdims=True)
    c = acc - mu
    var = jnp.mean(c * c, axis=-1, keepdims=True)
    y = c * lax.rsqrt(var + EPS) * lag_ref[...] + lab_ref[...]
    cat_ref[:, 0:ch] = _silu(y).astype(BF16)

    lane = lax.broadcasted_iota(jnp.int32, (CHUNK, LANE), 1)
    lo_half = lane < (LANE // 2)
    for c in range(tm // CHUNK):
        rows = slice(c * CHUNK, (c + 1) * CHUNK)
        for p in range(ch // LANE):
            cols = slice(p * LANE, (p + 1) * LANE)
            vv = v_ref[rows, cols]
            s0 = _dot(ws_ref[2 * p], vv)
            s1 = _dot(ws_ref[2 * p + 1], vv)
            sv = jnp.where(lo_half, s0, s1) + bs_ref[:, cols]
            bo = u_ref[rows, cols].astype(F32) * sv
            cat_ref[rows, ch + p * LANE:ch + (p + 1) * LANE] = bo.astype(BF16)

    y = _dot(cat_ref[...], wo_ref[...])
    o_ref[...] = x_ref[...] + mod_ref[0, 2:3, :] * y


def _mixer(a, u, v, x2, mod, mod_map, conv_w, conv_b, la_g, la_b, w_s, bias_map, w_out, tm, seq):
    n, d = x2.shape
    ch = a.shape[1]
    hb = tm // HALO
    nhb = n // HALO
    row = lambda t: (t, 0)
    const = lambda t: (0, 0)
    kern = functools.partial(_mixer_kernel, tiles_per_seq=seq // tm)
    return pl.pallas_call(
        kern,
        grid=(n // tm,),
        in_specs=[
            pl.BlockSpec((tm, ch), row),
            pl.BlockSpec((HALO, ch), lambda t: (jnp.maximum(t * hb - 1, 0), 0)),
            pl.BlockSpec((HALO, ch), lambda t: (jnp.minimum((t + 1) * hb, nhb - 1), 0)),
            pl.BlockSpec((tm, ch), row),
            pl.BlockSpec((tm, ch), row),
            pl.BlockSpec((tm, d), row),
            pl.BlockSpec((1, 3, d), mod_map),
            pl.BlockSpec(conv_w.shape, const),
            pl.BlockSpec((1, ch), const),
            pl.BlockSpec((1, ch), const),
            pl.BlockSpec((1, ch), const),
            pl.BlockSpec(w_s.shape, lambda t: (0, 0, 0)),
            pl.BlockSpec(bias_map.shape, const),
            pl.BlockSpec(w_out.shape, const),
        ],
        out_specs=pl.BlockSpec((tm, d), row),
        out_shape=jax.ShapeDtypeStruct((n, d), F32),
        scratch_shapes=[
            pltpu.VMEM((tm + 2 * HALO, ch), F32),
            pltpu.VMEM((7, tm + 24, ch), F32),
            pltpu.VMEM((tm, ch), F32),
            pltpu.VMEM((tm, 2 * ch), BF16),
        ],
        compiler_params=_cparams(("parallel",)),
        name="l0_mixer",
    )(a, a, a, u, v, x2, mod, conv_w, conv_b, la_g, la_b, w_s, bias_map, w_out)


def _ffn_kernel(x_ref, mod_ref, g_ref, w1_ref, w3_ref, w2_ref, o_ref, *, tf):
    x = x_ref[...]
    h = _rms_mod(x, g_ref[...], mod_ref[0, 0:1, :], mod_ref[0, 1:2, :]).astype(BF16)
    acc = None
    for c in range(w1_ref.shape[1] // tf):
        cols = slice(c * tf, (c + 1) * tf)
        act = _silu(_dot(h, w1_ref[:, cols])) * _dot(h, w3_ref[:, cols])
        part = _dot(act.astype(BF16), w2_ref[cols, :])
        acc = part if acc is None else acc + part
    o_ref[...] = x + mod_ref[0, 2:3, :] * acc


def _ffn(x2, mod, mod_map, g, w1, w3, w2, tm, tf):
    n, d = x2.shape
    row = lambda t: (t, 0)
    resident = lambda w: pl.BlockSpec(w.shape, lambda t: (0, 0), pipeline_mode=pl.Buffered(1))
    return pl.pallas_call(
        functools.partial(_ffn_kernel, tf=tf),
        grid=(n // tm,),
        in_specs=[
            pl.BlockSpec((tm, d), row),
            pl.BlockSpec((1, 3, d), mod_map),
            pl.BlockSpec((1, d), lambda t: (0, 0)),
            resident(w1), resident(w3), resident(w2),
        ],
        out_specs=pl.BlockSpec((tm, d), row),
        out_shape=jax.ShapeDtypeStruct((n, d), F32),
        compiler_params=_cparams(("parallel",)),
        name="ffn",
    )(x2, mod, g, w1, w3, w2)


def _rotate(t, cos, sin):
    return t * cos + pltpu.roll(t, LANE // 2, 1) * sin


def _mla_proj_kernel(x_ref, mod_ref, g_ref, win_ref, gq_ref, gkv_ref, wa_ref, wukv_ref, cos_ref, sin_ref,
                     q_ref, k_ref, v_ref):
    heads = q_ref.shape[1]
    h = _rms_mod(x_ref[...], g_ref[...], mod_ref[0, 0:1, :], mod_ref[0, 1:2, :]).astype(BF16)
    cos = cos_ref[...]
    sin = sin_ref[...]
    z = _dot(h, win_ref[...])
    kr = _rotate(z[:, Q_RANK + KV_RANK:], cos, sin).astype(BF16)
    qn = _rms(z[:, :Q_RANK], gq_ref[...]).astype(BF16)
    qa = _dot(qn, wa_ref[...])
    kvn = _rms(z[:, Q_RANK:Q_RANK + KV_RANK], gkv_ref[...]).astype(BF16)
    kv = _dot(kvn, wukv_ref[...])
    for hd in range(heads):
        q_ref[0, hd, :, 0:LANE] = (qa[:, hd * QK_PAD:hd * QK_PAD + LANE] * Q_SCALE).astype(BF16)
        qr = _rotate(qa[:, hd * QK_PAD + LANE:(hd + 1) * QK_PAD], cos, sin)
        q_ref[0, hd, :, LANE:QK_PAD] = (qr * Q_SCALE).astype(BF16)
        k_ref[0, hd, :, 0:LANE] = kv[:, hd * 2 * LANE:hd * 2 * LANE + LANE].astype(BF16)
        k_ref[0, hd, :, LANE:QK_PAD] = kr
        v_ref[0, hd, :, :] = kv[:, hd * 2 * LANE + LANE:(hd + 1) * 2 * LANE].astype(BF16)


def _mla_proj(x3, mod, mod_map, g, win, gq, gkv, wa, wukv, cos, sin, tm):
    b, t, d = x3.shape
    heads = wukv.shape[1] // (2 * LANE)
    tps = t // tm
    row = lambda i: (i // tps, i % tps, 0)
    const = lambda i: (0, 0)
    tab = lambda i: (i % tps, 0)
    hrow = lambda i: (i // tps, 0, i % tps, 0)
    return pl.pallas_call(
        _mla_proj_kernel,
        grid=(b * tps,),
        in_specs=[
            pl.BlockSpec((None, tm, d), row),
            pl.BlockSpec((1, 3, d), mod_map),
            pl.BlockSpec((1, d), const),
            pl.BlockSpec(win.shape, const),
            pl.BlockSpec(gq.shape, const),
            pl.BlockSpec(gkv.shape, const),
            pl.BlockSpec(wa.shape, const),
            pl.BlockSpec(wukv.shape, const),
            pl.BlockSpec((tm, LANE), tab),
            pl.BlockSpec((tm, LANE), tab),
        ],
        out_specs=[
            pl.BlockSpec((1, heads, tm, QK_PAD), hrow),
            pl.BlockSpec((1, heads, tm, QK_PAD), hrow),
            pl.BlockSpec((1, heads, tm, LANE), hrow),
        ],
        out_shape=[
            jax.ShapeDtypeStruct((b, heads, t, QK_PAD), BF16),
            jax.ShapeDtypeStruct((b, heads, t, QK_PAD), BF16),
            jax.ShapeDtypeStruct((b, heads, t, LANE), BF16),
        ],
        compiler_params=_cparams(("parallel",)),
        name="l1_proj",
    )(x3, mod, g, win, gq, gkv, wa, wukv, cos, sin)


def _attn_kernel(q_ref, kc_ref, vc_ref, kx_ref, vx_ref, o_ref, *, tk):
    q = q_ref[0, 0]
    tq = q.shape[0]

    def step(k, v, m, l, acc):
        s = lax.dot_general(q, k, (((1,), (1,)), ((), ())), preferred_element_type=F32)
        m_new = jnp.maximum(m, jnp.max(s, axis=-1, keepdims=True))
        alpha = jnp.exp2(m - m_new)
        p = jnp.exp2(s - m_new)
        l = alpha * l + jnp.sum(p, axis=-1, keepdims=True)
        acc = alpha * acc + _dot(p.astype(BF16), v)
        return m_new, l, acc

    m = jnp.full((tq, 1), -jnp.inf, F32)
    l = jnp.zeros((tq, 1), F32)
    acc = jnp.zeros((tq, vc_ref.shape[-1]), F32)
    m, l, acc = step(kc_ref[0, 0], vc_ref[0, 0], m, l, acc)
    for c in range(kx_ref.shape[2] // tk):
        m, l, acc = step(kx_ref[0, 0, c * tk:(c + 1) * tk, :], vx_ref[0, 0, c * tk:(c + 1) * tk, :],
                         m, l, acc)
    o_ref[0] = (acc / l).astype(BF16)


def _attention(q, kc, vc, kx, vx, tq, tk):
    b, heads, t, _ = q.shape
    tc = kc.shape[2]
    dv = vx.shape[-1]
    return pl.pallas_call(
        functools.partial(_attn_kernel, tk=tk),
        grid=(b, heads, t // tq),
        in_specs=[
            pl.BlockSpec((1, 1, tq, QK_PAD), lambda i, h, j: (i, h, j, 0)),
            pl.BlockSpec((1, 1, tc, QK_PAD), lambda i, h, j: (i, h, 0, 0)),
            pl.BlockSpec((1, 1, tc, dv), lambda i, h, j: (i, h, 0, 0)),
            pl.BlockSpec((1, 1, t, QK_PAD), lambda i, h, j: (i, h, 0, 0)),
            pl.BlockSpec((1, 1, t, dv), lambda i, h, j: (i, h, 0, 0)),
        ],
        out_specs=pl.BlockSpec((1, tq, dv), lambda i, h, j: (i, j, h)),
        out_shape=jax.ShapeDtypeStruct((b, t, heads * dv), BF16),
        compiler_params=_cparams(("parallel", "parallel", "arbitrary")),
        name="l1_attn",
    )(q, kc, vc, kx, vx)


def _outproj_router_kernel(a_ref, w_ref, x_ref, moda_ref, modf_ref, g_ref, rh_ref, rl_ref,
                           xo_ref, h_ref, sel_ref):
    tm = x_ref.shape[0]
    sub = tm // ROW_SPLIT
    for blk in range(ROW_SPLIT):
        rows = slice(blk * sub, (blk + 1) * sub)
        x = x_ref[rows, :] + moda_ref[0, 2:3, :] * _dot(a_ref[rows, :], w_ref[...])
        xo_ref[rows, :] = x
        h = _rms_mod(x, g_ref[...], modf_ref[0, 0:1, :], modf_ref[0, 1:2, :])
        h_ref[rows, :] = h
        hh = h.astype(BF16)
        hl = (h - hh.astype(F32)).astype(BF16)
        rh = rh_ref[...]
        logits = _dot(hh, rh) + (_dot(hl, rh) + _dot(hh, rl_ref[...]))
        lane = lax.broadcasted_iota(jnp.int32, logits.shape, 1).astype(F32)
        neg = -jnp.inf
        logits = jnp.where(lane < N_EXPERTS, logits, neg)
        v1 = jnp.max(logits, axis=-1, keepdims=True)
        i1 = jnp.min(jnp.where(logits == v1, lane, float(LANE)), axis=-1, keepdims=True)
        rest = jnp.where(lane == i1, neg, logits)
        v2 = jnp.max(rest, axis=-1, keepdims=True)
        i2 = jnp.min(jnp.where(rest == v2, lane, float(LANE)), axis=-1, keepdims=True)
        e2 = jnp.exp(v2 - v1)
        w1 = 1.0 / (1.0 + e2)
        w2 = e2 / (1.0 + e2)
        sel_ref[rows, :] = (jnp.where(lane == 0.0, i1, 0.0) + jnp.where(lane == 1.0, i2, 0.0)
                            + jnp.where(lane == 2.0, w1, 0.0) + jnp.where(lane == 3.0, w2, 0.0))


def _outproj_router(a, w, x2, mod_a, mod_f, mod_map, g, r_hi, r_lo, tm):
    n, d = x2.shape
    row = lambda t: (t, 0)
    const = lambda t: (0, 0)
    return pl.pallas_call(
        _outproj_router_kernel,
        grid=(n // tm,),
        in_specs=[
            pl.BlockSpec((tm, a.shape[1]), row),
            pl.BlockSpec(w.shape, const),
            pl.BlockSpec((tm, d), row),
            pl.BlockSpec((1, 3, d), mod_map),
            pl.BlockSpec((1, 3, d), mod_map),
            pl.BlockSpec((1, d), const),
            pl.BlockSpec(r_hi.shape, const),
            pl.BlockSpec(r_lo.shape, const),
        ],
        out_specs=[pl.BlockSpec((tm, d), row), pl.BlockSpec((tm, d), row), pl.BlockSpec((tm, LANE), row)],
        out_shape=[jax.ShapeDtypeStruct((n, d), F32), jax.ShapeDtypeStruct((n, d), F32),
                   jax.ShapeDtypeStruct((n, LANE), F32)],
        compiler_params=_cparams(("parallel",)),
        name="l1_outproj_router",
    )(a, w, x2, mod_a, mod_f, g, r_hi, r_lo)


def _routing_tables(sel, tm):
    n = sel.shape[0]
    ids = sel[:, 0:2].astype(jnp.int32).reshape(2 * n)
    onehot = (ids[:, None] == jnp.arange(N_EXPERTS, dtype=jnp.int32)[None, :]).astype(jnp.int32)
    csum = jnp.cumsum(onehot, axis=0)
    rank = jnp.sum(csum * onehot, axis=1) - 1
    tiles = (csum[-1] + tm - 1) // tm
    tile_end = jnp.cumsum(tiles)
    start = (tile_end - tiles) * tm
    pos = start[ids] + rank
    n_tiles = (2 * n) // tm + N_EXPERTS
    tile_expert = jnp.sum(jnp.arange(n_tiles, dtype=jnp.int32)[:, None] >= tile_end[None, :], axis=1)
    tile_expert = jnp.minimum(tile_expert, N_EXPERTS - 1).astype(jnp.int32)
    n_pad = n_tiles * tm - 2 * n
    region_start = jnp.concatenate([start + csum[-1], tile_end[-1:] * tm])
    region_size = jnp.concatenate([tiles * tm - csum[-1], n_tiles * tm - tile_end[-1:] * tm])
    region_end = jnp.cumsum(region_size)
    q = jnp.arange(n_pad, dtype=jnp.int32)
    region = jnp.sum(q[:, None] >= region_end[None, :], axis=1)
    pad_slots = region_start[region] + q - (region_end - region_size)[region]
    return (pos.astype(jnp.int32), pad_slots.astype(jnp.int32).reshape(1, n_pad), tile_expert,
            tile_end[-1:].astype(jnp.int32), n_tiles)


def _row_copy(src, src_row, dst, dst_row, sem):
    return pltpu.make_async_copy(src.at[pl.ds(src_row, 1)], dst.at[pl.ds(dst_row, 1)], sem)


def _scatter_kernel(pos_ref, pad_ref, h_ref, hs_ref, zero_ref, sem):
    tm = h_ref.shape[0]
    n_pad = pad_ref.shape[1]

    def drain(n_rows):
        for _ in range(n_rows // tm):
            pltpu.make_async_copy(h_ref, hs_ref.at[pl.ds(0, tm)], sem).wait()

    @pl.when(pl.program_id(0) == 0)
    def _():
        zero_ref[...] = jnp.zeros_like(zero_ref)

        def issue_pad(q, carry):
            _row_copy(zero_ref, 0, hs_ref, pad_ref[0, q], sem).start()
            return carry

        lax.fori_loop(0, n_pad, issue_pad, 0, unroll=8)
        drain(n_pad)

    def issue(r, carry):
        _row_copy(h_ref, r, hs_ref, pos_ref[0, 2 * r], sem).start()
        _row_copy(h_ref, r, hs_ref, pos_ref[0, 2 * r + 1], sem).start()
        return carry

    lax.fori_loop(0, tm, issue, 0, unroll=8)
    drain(2 * tm)


def _scatter_rows(pos2, pad_slots, hp, n_slots, tm):
    n, w = hp.shape
    assert pad_slots.shape[1] % tm == 0
    return pl.pallas_call(
        _scatter_kernel,
        grid=(n // tm,),
        in_specs=[
            pl.BlockSpec((None, 1, 2 * tm), lambda t: (t, 0, 0), memory_space=pltpu.SMEM),
            pl.BlockSpec(pad_slots.shape, lambda t: (0, 0), memory_space=pltpu.SMEM),
            pl.BlockSpec((tm, w), lambda t: (t, 0)),
        ],
        out_specs=pl.BlockSpec(memory_space=pl.ANY),
        out_shape=jax.ShapeDtypeStruct((n_slots, w), hp.dtype),
        scratch_shapes=[pltpu.VMEM((8, w), hp.dtype), pltpu.SemaphoreType.DMA(())],
        compiler_params=_cparams(("arbitrary",)),
        name="moe_scatter",
    )(pos2, pad_slots, hp)


def _moe_ffn_kernel(te_ref, nt_ref, hs_ref, w1_ref, w3_ref, w2_ref, y_ref, *, tf):
    del te_ref
    valid = pl.program_id(0) < nt_ref[0]

    @pl.when(valid)
    def _():
        h = hs_ref[...].astype(BF16)
        acc = None
        for c in range(w1_ref.shape[2] // tf):
            cols = slice(c * tf, (c + 1) * tf)
            act = _silu(_dot(h, w1_ref[0, :, cols])) * _dot(h, w3_ref[0, :, cols])
            part = _dot(act.astype(BF16), w2_ref[0, cols, :])
            acc = part if acc is None else acc + part
        y_ref[...] = acc

    @pl.when(jnp.logical_not(valid))
    def _():
        y_ref[...] = jnp.zeros_like(y_ref)


def _moe_ffn(tile_expert, n_tiles_used, hs, w1, w3, w2, tm, tf):
    p, w = hs.shape
    _, d, dff = w1.shape
    expert = lambda j, te, nt: (te[j], 0, 0)
    resident = lambda shape: pl.BlockSpec(shape, expert, pipeline_mode=pl.Buffered(1))
    grid_spec = pltpu.PrefetchScalarGridSpec(
        num_scalar_prefetch=2,
        grid=(p // tm,),
        in_specs=[
            pl.BlockSpec((tm, w), lambda j, te, nt: (j, 0)),
            resident((1, d, dff)), resident((1, d, dff)), resident((1, dff, d)),
        ],
        out_specs=pl.BlockSpec((tm, w), lambda j, te, nt: (j, 0)),
    )
    return pl.pallas_call(
        functools.partial(_moe_ffn_kernel, tf=tf),
        grid_spec=grid_spec,
        out_shape=jax.ShapeDtypeStruct((p, w), F32),
        compiler_params=_cparams(("arbitrary",)),
        name="moe_ffn",
    )(tile_expert, n_tiles_used, hs, w1, w3, w2)


def _combine_kernel(pos_ref, nxt_ref, x_ref, mod_ref, sel_ref, fg_ref, ys_ref, o_ref, buf, sem):
    tm = x_ref.shape[0]
    t = pl.program_id(0)
    slot = t % 2

    def gather(p_ref, s):
        def issue(r, carry):
            _row_copy(ys_ref, p_ref[0, 2 * r], buf.at[s, 0], r, sem.at[s]).start()
            _row_copy(ys_ref, p_ref[0, 2 * r + 1], buf.at[s, 1], r, sem.at[s]).start()
            return carry

        lax.fori_loop(0, tm, issue, 0, unroll=8)

    @pl.when(t == 0)
    def _():
        gather(pos_ref, 0)

    @pl.when(t + 1 < pl.num_programs(0))
    def _():
        gather(nxt_ref, 1 - slot)

    for k in range(2):
        pltpu.make_async_copy(ys_ref.at[pl.ds(0, tm)], buf.at[slot, k], sem.at[slot]).wait()
    sel = sel_ref[...]
    y = sel[:, 2:3] * buf[slot, 0] + sel[:, 3:4] * buf[slot, 1]
    o_ref[...] = _rms(x_ref[...] + mod_ref[0, 2:3, :] * y, fg_ref[...])


def _combine(pos2, x2, mod, mod_map, sel, final_g, ys, tm):
    n, d = x2.shape
    w = ys.shape[1]
    steps = n // tm
    row = lambda t: (t, 0)
    return pl.pallas_call(
        _combine_kernel,
        grid=(steps,),
        in_specs=[
            pl.BlockSpec((None, 1, 2 * tm), lambda t: (t, 0, 0), memory_space=pltpu.SMEM),
            pl.BlockSpec((None, 1, 2 * tm), lambda t: (jnp.minimum(t + 1, steps - 1), 0, 0),
                         memory_space=pltpu.SMEM),
            pl.BlockSpec((tm, d), row),
            pl.BlockSpec((1, 3, d), mod_map),
            pl.BlockSpec((tm, LANE), row),
            pl.BlockSpec((1, d), lambda t: (0, 0)),
            pl.BlockSpec(memory_space=pl.ANY),
        ],
        out_specs=pl.BlockSpec((tm, d), row),
        out_shape=jax.ShapeDtypeStruct((n, d), F32),
        scratch_shapes=[pltpu.VMEM((2, 2, tm, w), F32), pltpu.SemaphoreType.DMA((2,))],
        compiler_params=_cparams(("arbitrary",)),
        name="moe_combine",
    )(pos2, pos2, x2, mod, sel, final_g, ys)


def _rope_tables(t):
    freqs = QK_ROPE // 4
    rows = t // GRID_W
    row = jnp.repeat(jnp.arange(rows), GRID_W).astype(F32)
    col = jnp.tile(jnp.arange(GRID_W), rows).astype(F32)
    inv_freq = ROPE_BASE ** (-jnp.arange(freqs, dtype=F32) / freqs)
    ang = jnp.stack([row[:, None] * inv_freq, col[:, None] * inv_freq], axis=1)
    cos = jnp.cos(ang)
    sin = jnp.sin(ang)
    c64 = jnp.stack([cos, cos], axis=2).reshape(t, QK_ROPE)
    s64 = jnp.stack([-sin, sin], axis=2).reshape(t, QK_ROPE)
    pad = jnp.zeros((t, LANE - QK_ROPE), F32)
    return jnp.concatenate([c64, pad], axis=1), jnp.concatenate([s64, pad], axis=1)


def _rope_partner_index():
    freqs = QK_ROPE // 4
    d = jnp.arange(QK_ROPE)
    half = (d // freqs) % 2
    return jnp.where(half == 0, d + freqs, d - freqs)


def _mla_weights(o_w_in, o_w_uq, o_w_ukv):
    heads = o_w_ukv.shape[1] // (QK_NOPE + V_HD)
    pidx = _rope_partner_index()
    pe = o_w_in[:, Q_RANK + KV_RANK:]
    win = jnp.concatenate([o_w_in, pe[:, pidx]], axis=1).astype(BF16)
    uq = o_w_uq.reshape(Q_RANK, heads, QK_NOPE + QK_ROPE)
    nope, rope = uq[..., :QK_NOPE], uq[..., QK_NOPE:]
    wa = jnp.concatenate([nope, rope, rope[..., pidx]], axis=-1).reshape(Q_RANK, heads * QK_PAD).astype(BF16)
    return win, wa, o_w_ukv.astype(BF16)


def kernel(x, c, ctx, c_ctx, w_ada, b_ada, norm_g, e_w_in, e_conv_w, e_conv_b, e_ln_a_g, e_ln_a_b, e_ln_v_g, e_ln_v_b, e_w_s, e_b_s, e_w_out, e_ffn_w1, e_ffn_w3, e_ffn_w2, o_w_in, o_g_q, o_g_kv, o_w_uq, o_w_ukv, o_w_o, o_router, o_exp_w1, o_exp_w3, o_exp_w2, final_g):
    b, t, d = x.shape
    tc = ctx.shape[1]
    n, nc = b * t, b * tc
    depth = w_ada.shape[0]
    assert depth == 2 and t % GRID_W == 0 and t % CHUNK == 0 and tc % CHUNK == 0

    tm_x = min(512, t)
    tm_c = min(256, tc)

    rows = 16
    cc = jnp.concatenate([c, c_ctx[None], jnp.zeros((rows - b - 1, d), F32)], axis=0)
    mods = _ada(cc, w_ada, b_ada).reshape(depth, rows, 6, d)
    xmap = lambda tm: (lambda i: (i // (t // tm), 0, 0))
    cmap = lambda i: (b, 0, 0)

    x2 = x.reshape(n, d)
    c2 = ctx.reshape(nc, d)
    row1 = lambda v: v.reshape(1, -1)

    w_in = e_w_in[0].astype(BF16)
    w_out = e_w_out[0].astype(BF16)
    w_s = e_w_s[0].astype(BF16)
    ch = e_conv_w.shape[-1]
    bias_map = jnp.repeat(e_b_s[0].T, ch // B_HEADS, axis=1)
    conv_w = jnp.concatenate([e_conv_w[0], jnp.zeros((1, ch), F32)], axis=0)
    f1 = e_ffn_w1[0].astype(BF16)
    f3 = e_ffn_w3[0].astype(BF16)
    f2 = e_ffn_w2[0].astype(BF16)
    dff = f1.shape[-1]
    tf = 2 * LANE
    assert dff % tf == 0

    def layer0(s2, seq, tm, mod_a, mod_f, mod_map, tm_f):
        a, u, v = _inproj(s2, mod_a, mod_map(tm), row1(norm_g[0, 0]), w_in,
                          row1(e_ln_v_g[0]), row1(e_ln_v_b[0]), tm)
        s2 = _mixer(a, u, v, s2, mod_a, mod_map(tm), conv_w, row1(e_conv_b[0]),
                    row1(e_ln_a_g[0]), row1(e_ln_a_b[0]), w_s, bias_map, w_out, tm, seq)
        return _ffn(s2, mod_f, mod_map(tm_f), row1(norm_g[0, 1]), f1, f3, f2, tm_f, tf)

    m0a, m0f = mods[0, :, 0:3], mods[0, :, 3:6]
    x2 = layer0(x2, t, tm_x, m0a, m0f, xmap, tm_x)
    c2 = layer0(c2, tc, tm_c, m0a, m0f, lambda tm: cmap, tm_c)

    m1a, m1f = mods[1, :, 0:3], mods[1, :, 3:6]
    win, wa, wukv = _mla_weights(o_w_in[0], o_w_uq[0], o_w_ukv[0])
    cos, sin = _rope_tables(t)
    ones = jnp.concatenate([jnp.ones((tc, QK_ROPE), F32), jnp.zeros((tc, LANE - QK_ROPE), F32)], axis=1)
    zeros = jnp.zeros((tc, LANE), F32)
    g10 = row1(norm_g[1, 0])
    proj = functools.partial(_mla_proj, g=g10, win=win, gq=row1(o_g_q[0]),
                             gkv=row1(o_g_kv[0]), wa=wa, wukv=wukv)
    q_x, k_x, v_x = proj(x2.reshape(b, t, d), m1a, xmap(tm_x), cos=cos, sin=sin, tm=tm_x)
    _, k_c, v_c = proj(c2.reshape(b, tc, d), m1a, cmap, cos=ones, sin=zeros, tm=tm_c)
    att = _attention(q_x, k_c, v_c, k_x, v_x, tq=min(2048, t), tk=min(256, t))

    r_pad = jnp.concatenate([o_router[0], jnp.zeros((d, LANE - N_EXPERTS), F32)], axis=1)
    r_hi = r_pad.astype(BF16)
    r_lo = (r_pad - r_hi.astype(F32)).astype(BF16)
    x2, hp, sel = _outproj_router(att.reshape(n, -1), o_w_o[0].astype(BF16), x2, m1a, m1f, xmap(tm_x),
                                  row1(norm_g[1, 1]), r_hi, r_lo, tm_x)
    tm_e = 512
    pos, pad_slots, tile_expert, n_used, n_tiles = _routing_tables(sel, tm_e)
    hs = _scatter_rows(pos.reshape(n // tm_x, 1, 2 * tm_x), pad_slots, hp, n_tiles * tm_e, tm_x)
    dexp = o_exp_w1.shape[-1]
    tf_e = 2 * LANE
    assert dexp % tf_e == 0
    ys = _moe_ffn(tile_expert, n_used, hs, o_exp_w1[0].astype(BF16), o_exp_w3[0].astype(BF16),
                  o_exp_w2[0].astype(BF16), tm_e, tf_e)
    tm_k = min(256, t)
    out = _combine(pos.reshape(n // tm_k, 1, 2 * tm_k), x2, m1f, xmap(tm_k), sel, row1(final_g), ys, tm_k)
    return out.reshape(b, t, d)
```

```python
import functools

import jax
import jax.numpy as jnp
from jax import lax
from jax.experimental import pallas as pl
from jax.experimental.pallas import tpu as pltpu

F32 = jnp.float32
BF16 = jnp.bfloat16

EPS = 1e-6
GRID_W = 64
CONV_K = 31
CHUNK = 128
B_HEADS = 8
QK_NOPE = 128
QK_ROPE = 64
V_HD = 128
Q_RANK = 384
KV_RANK = 256
ROPE_BASE = 10000.0
N_EXPERTS = 8
ATTN_SCALE = (QK_NOPE + QK_ROPE) ** -0.5
LOG2_E = 1.4426950408889634
Q_SCALE = ATTN_SCALE * LOG2_E

LANE = 128
HALO = 16
CONV_ROWS = 64
QK_PAD = 256
ROW_SPLIT = 2
VMEM_LIMIT = 52 * 1024 * 1024


def _cparams(sem):
    return pltpu.CompilerParams(dimension_semantics=sem, vmem_limit_bytes=VMEM_LIMIT)


def _silu(x):
    return x * jax.nn.sigmoid(x)


def _rms_mod(x, g, shift, scale):
    y = x * lax.rsqrt(jnp.mean(x * x, axis=-1, keepdims=True) + EPS) * g
    return y * (1.0 + scale) + shift


def _rms(x, g):
    return x * lax.rsqrt(jnp.mean(x * x, axis=-1, keepdims=True) + EPS) * g


def _dot(a, b):
    return jnp.dot(a, b, preferred_element_type=F32)


def _ada_kernel(cc_ref, w_ref, b_ref, o_ref):
    s = _silu(cc_ref[...]).astype(BF16)
    o_ref[0] = _dot(s, w_ref[0].astype(BF16)) + b_ref[0]


def _ada(cc, w_ada, b_ada):
    depth, d, d6 = w_ada.shape
    rows = cc.shape[0]
    tn = 1536
    return pl.pallas_call(
        _ada_kernel,
        grid=(depth, d6 // tn),
        in_specs=[
            pl.BlockSpec((rows, d), lambda i, j: (0, 0)),
            pl.BlockSpec((1, d, tn), lambda i, j: (i, 0, j)),
            pl.BlockSpec((1, 1, tn), lambda i, j: (i, 0, j)),
        ],
        out_specs=pl.BlockSpec((1, rows, tn), lambda i, j: (i, 0, j)),
        out_shape=jax.ShapeDtypeStruct((depth, rows, d6), F32),
        compiler_params=_cparams(("parallel", "parallel")),
        name="ada",
    )(cc, w_ada, b_ada.reshape(depth, 1, d6))


def _inproj_kernel(x_ref, mod_ref, g_ref, w_ref, lng_ref, lnb_ref, a_ref, u_ref, v_ref):
    ch = a_ref.shape[1]
    h = _rms_mod(x_ref[...], g_ref[...], mod_ref[0, 0:1, :], mod_ref[0, 1:2, :]).astype(BF16)
    z = _dot(h, w_ref[...])
    a = z[:, 0:ch] * jax.nn.sigmoid(z[:, ch:2 * ch])
    u = jax.nn.gelu(z[:, 2 * ch:3 * ch])
    v = jax.nn.gelu(z[:, 3 * ch:4 * ch])
    mu = jnp.mean(v, axis=-1, keepdims=True)
    vc = v - mu
    var = jnp.mean(vc * vc, axis=-1, keepdims=True)
    v = vc * lax.rsqrt(var + EPS) * lng_ref[...] + lnb_ref[...]
    a_ref[...] = a.astype(BF16)
    u_ref[...] = u.astype(BF16)
    v_ref[...] = v.astype(BF16)


def _inproj(x2, mod, mod_map, g, w_in, ln_g, ln_b, tm):
    n, d = x2.shape
    ch = w_in.shape[1] // 4
    row = lambda t: (t, 0)
    const = lambda t: (0, 0)
    out = jax.ShapeDtypeStruct((n, ch), BF16)
    return pl.pallas_call(
        _inproj_kernel,
        grid=(n // tm,),
        in_specs=[
            pl.BlockSpec((tm, d), row),
            pl.BlockSpec((1, 3, d), mod_map),
            pl.BlockSpec((1, d), const),
            pl.BlockSpec(w_in.shape, const),
            pl.BlockSpec((1, ch), const),
            pl.BlockSpec((1, ch), const),
        ],
        out_specs=[pl.BlockSpec((tm, ch), row)] * 3,
        out_shape=[out, out, out],
        compiler_params=_cparams(("parallel",)),
        name="l0_inproj",
    )(x2, mod, g, w_in, ln_g, ln_b)


def _mixer_kernel(a_ref, ap_ref, an_ref, u_ref, v_ref, x_ref, mod_ref, cw_ref, cb_ref,
                  lag_ref, lab_ref, ws_ref, bs_ref, wo_ref, o_ref,
                  ext_ref, sh_ref, conv_ref, cat_ref, *, tiles_per_seq):
    tm, ch = a_ref.shape
    t = pl.program_id(0)
    first = (t % tiles_per_seq) == 0
    last = (t % tiles_per_seq) == tiles_per_seq - 1

    ext_ref[0:HALO, :] = jnp.where(first, 0.0, ap_ref[...].astype(F32))
    ext_ref[HALO:HALO + tm, :] = a_ref[...].astype(F32)
    ext_ref[HALO + tm:, :] = jnp.where(last, 0.0, an_ref[...].astype(F32))
    span = tm + 24
    for r in range(1, 8):
        sh_ref[r - 1] = ext_ref[r:r + span, :]

    off0 = HALO - CONV_K // 2
    cb = cb_ref[...]

    def conv_block(rb, carry):
        base = pl.multiple_of(rb * CONV_ROWS, CONV_ROWS)
        acc = jnp.broadcast_to(cb, (CONV_ROWS, ch))
        for k in range(CONV_K):
            q, r = divmod(off0 + k, 8)
            start = pl.multiple_of(base + 8 * q, 8)
            if r == 0:
                win = ext_ref[pl.ds(start, CONV_ROWS), :]
            else:
                win = sh_ref[r - 1, pl.ds(start, CONV_ROWS), :]
            acc = acc + cw_ref[k:k + 1, :] * win
        conv_ref[pl.ds(base, CONV_ROWS), :] = acc
        return carry

    lax.fori_loop(0, tm // CONV_ROWS, conv_block, 0)

    acc = conv_ref[...]
    mu = jnp.mean(acc, axis=-1, keepdims=True)
    c = acc - mu
    var = jnp.mean(c * c, axis=-1, keepdims=True)
    y = c * lax.rsqrt(var + EPS) * lag_ref[...] + lab_ref[...]
    cat_ref[:, 0:ch] = _silu(y).astype(BF16)

    lane = lax.broadcasted_iota(jnp.int32, (CHUNK, LANE), 1)
    lo_half = lane < (LANE // 2)
    for c in range(tm // CHUNK):
        rows = slice(c * CHUNK, (c + 1) * CHUNK)
        for p in range(ch // LANE):
            cols = slice(p * LANE, (p + 1) * LANE)
            vv = v_ref[rows, cols]
            s0 = _dot(ws_ref[2 * p], vv)
            s1 = _dot(ws_ref[2 * p + 1], vv)
            sv = jnp.where(lo_half, s0, s1) + bs_ref[:, cols]
            bo = u_ref[rows, cols].astype(F32) * sv
            cat_ref[rows, ch + p * LANE:ch + (p + 1) * LANE] = bo.astype(BF16)

    y = _dot(cat_ref[...], wo_ref[...])
    o_ref[...] = x_ref[...] + mod_ref[0, 2:3, :] * y


def _mixer(a, u, v, x2, mod, mod_map, conv_w, conv_b, la_g, la_b, w_s, bias_map, w_out, tm, seq):
    n, d = x2.shape
    ch = a.shape[1]
    hb = tm // HALO
    nhb = n // HALO
    row = lambda t: (t, 0)
    const = lambda t: (0, 0)
    kern = functools.partial(_mixer_kernel, tiles_per_seq=seq // tm)
    return pl.pallas_call(
        kern,
        grid=(n // tm,),
        in_specs=[
            pl.BlockSpec((tm, ch), row),
            pl.BlockSpec((HALO, ch), lambda t: (jnp.maximum(t * hb - 1, 0), 0)),
            pl.BlockSpec((HALO, ch), lambda t: (jnp.minimum((t + 1) * hb, nhb - 1), 0)),
            pl.BlockSpec((tm, ch), row),
            pl.BlockSpec((tm, ch), row),
            pl.BlockSpec((tm, d), row),
            pl.BlockSpec((1, 3, d), mod_map),
            pl.BlockSpec(conv_w.shape, const),
            pl.BlockSpec((1, ch), const),
            pl.BlockSpec((1, ch), const),
            pl.BlockSpec((1, ch), const),
            pl.BlockSpec(w_s.shape, lambda t: (0, 0, 0)),
            pl.BlockSpec(bias_map.shape, const),
            pl.BlockSpec(w_out.shape, const),
        ],
        out_specs=pl.BlockSpec((tm, d), row),
        out_shape=jax.ShapeDtypeStruct((n, d), F32),
        scratch_shapes=[
            pltpu.VMEM((tm + 2 * HALO, ch), F32),
            pltpu.VMEM((7, tm + 24, ch), F32),
            pltpu.VMEM((tm, ch), F32),
            pltpu.VMEM((tm, 2 * ch), BF16),
        ],
        compiler_params=_cparams(("parallel",)),
        name="l0_mixer",
    )(a, a, a, u, v, x2, mod, conv_w, conv_b, la_g, la_b, w_s, bias_map, w_out)


def _ffn_kernel(x_ref, mod_ref, g_ref, w1_ref, w3_ref, w2_ref, o_ref, *, tf):
    x = x_ref[...]
    h = _rms_mod(x, g_ref[...], mod_ref[0, 0:1, :], mod_ref[0, 1:2, :]).astype(BF16)
    acc = None
    for c in range(w1_ref.shape[1] // tf):
        cols = slice(c * tf, (c + 1) * tf)
        act = _silu(_dot(h, w1_ref[:, cols])) * _dot(h, w3_ref[:, cols])
        part = _dot(act.astype(BF16), w2_ref[cols, :])
        acc = part if acc is None else acc + part
    o_ref[...] = x + mod_ref[0, 2:3, :] * acc


def _ffn(x2, mod, mod_map, g, w1, w3, w2, tm, tf):
    n, d = x2.shape
    row = lambda t: (t, 0)
    resident = lambda w: pl.BlockSpec(w.shape, lambda t: (0, 0), pipeline_mode=pl.Buffered(1))
    return pl.pallas_call(
        functools.partial(_ffn_kernel, tf=tf),
        grid=(n // tm,),
        in_specs=[
            pl.BlockSpec((tm, d), row),
            pl.BlockSpec((1, 3, d), mod_map),
            pl.BlockSpec((1, d), lambda t: (0, 0)),
            resident(w1), resident(w3), resident(w2),
        ],
        out_specs=pl.BlockSpec((tm, d), row),
        out_shape=jax.ShapeDtypeStruct((n, d), F32),
        compiler_params=_cparams(("parallel",)),
        name="ffn",
    )(x2, mod, g, w1, w3, w2)


def _rotate(t, cos, sin):
    return t * cos + pltpu.roll(t, LANE // 2, 1) * sin


def _mla_proj_kernel(x_ref, mod_ref, g_ref, win_ref, gq_ref, gkv_ref, wa_ref, wukv_ref, cos_ref, sin_ref,
                     q_ref, k_ref, v_ref):
    heads = q_ref.shape[1]
    h = _rms_mod(x_ref[...], g_ref[...], mod_ref[0, 0:1, :], mod_ref[0, 1:2, :]).astype(BF16)
    cos = cos_ref[...]
    sin = sin_ref[...]
    z = _dot(h, win_ref[...])
    kr = _rotate(z[:, Q_RANK + KV_RANK:], cos, sin).astype(BF16)
    qn = _rms(z[:, :Q_RANK], gq_ref[...]).astype(BF16)
    qa = _dot(qn, wa_ref[...])
    kvn = _rms(z[:, Q_RANK:Q_RANK + KV_RANK], gkv_ref[...]).astype(BF16)
    kv = _dot(kvn, wukv_ref[...])
    for hd in range(heads):
        q_ref[0, hd, :, 0:LANE] = (qa[:, hd * QK_PAD:hd * QK_PAD + LANE] * Q_SCALE).astype(BF16)
        qr = _rotate(qa[:, hd * QK_PAD + LANE:(hd + 1) * QK_PAD], cos, sin)
        q_ref[0, hd, :, LANE:QK_PAD] = (qr * Q_SCALE).astype(BF16)
        k_ref[0, hd, :, 0:LANE] = kv[:, hd * 2 * LANE:hd * 2 * LANE + LANE].astype(BF16)
        k_ref[0, hd, :, LANE:QK_PAD] = kr
        v_ref[0, hd, :, :] = kv[:, hd * 2 * LANE + LANE:(hd + 1) * 2 * LANE].astype(BF16)


def _mla_proj(x3, mod, mod_map, g, win, gq, gkv, wa, wukv, cos, sin, tm):
    b, t, d = x3.shape
    heads = wukv.shape[1] // (2 * LANE)
    tps = t // tm
    row = lambda i: (i // tps, i % tps, 0)
    const = lambda i: (0, 0)
    tab = lambda i: (i % tps, 0)
    hrow = lambda i: (i // tps, 0, i % tps, 0)
    return pl.pallas_call(
        _mla_proj_kernel,
        grid=(b * tps,),
        in_specs=[
            pl.BlockSpec((None, tm, d), row),
            pl.BlockSpec((1, 3, d), mod_map),
            pl.BlockSpec((1, d), const),
            pl.BlockSpec(win.shape, const),
            pl.BlockSpec(gq.shape, const),
            pl.BlockSpec(gkv.shape, const),
            pl.BlockSpec(wa.shape, const),
            pl.BlockSpec(wukv.shape, const),
            pl.BlockSpec((tm, LANE), tab),
            pl.BlockSpec((tm, LANE), tab),
        ],
        out_specs=[
            pl.BlockSpec((1, heads, tm, QK_PAD), hrow),
            pl.BlockSpec((1, heads, tm, QK_PAD), hrow),
            pl.BlockSpec((1, heads, tm, LANE), hrow),
        ],
        out_shape=[
            jax.ShapeDtypeStruct((b, heads, t, QK_PAD), BF16),
            jax.ShapeDtypeStruct((b, heads, t, QK_PAD), BF16),
            jax.ShapeDtypeStruct((b, heads, t, LANE), BF16),
        ],
        compiler_params=_cparams(("parallel",)),
        name="l1_proj",
    )(x3, mod, g, win, gq, gkv, wa, wukv, cos, sin)


def _attn_kernel(q_ref, kc_ref, vc_ref, kx_ref, vx_ref, o_ref, *, tk):
    q = q_ref[0, 0]
    tq = q.shape[0]

    def step(k, v, m, l, acc):
        s = lax.dot_general(q, k, (((1,), (1,)), ((), ())), preferred_element_type=F32)
        m_new = jnp.maximum(m, jnp.max(s, axis=-1, keepdims=True))
        alpha = jnp.exp2(m - m_new)
        p = jnp.exp2(s - m_new)
        l = alpha * l + jnp.sum(p, axis=-1, keepdims=True)
        acc = alpha * acc + _dot(p.astype(BF16), v)
        return m_new, l, acc

    m = jnp.full((tq, 1), -jnp.inf, F32)
    l = jnp.zeros((tq, 1), F32)
    acc = jnp.zeros((tq, vc_ref.shape[-1]), F32)
    m, l, acc = step(kc_ref[0, 0], vc_ref[0, 0], m, l, acc)
    for c in range(kx_ref.shape[2] // tk):
        m, l, acc = step(kx_ref[0, 0, c * tk:(c + 1) * tk, :], vx_ref[0, 0, c * tk:(c + 1) * tk, :],
                         m, l, acc)
    o_ref[0] = (acc / l).astype(BF16)


def _attention(q, kc, vc, kx, vx, tq, tk):
    b, heads, t, _ = q.shape
    tc = kc.shape[2]
    dv = vx.shape[-1]
    return pl.pallas_call(
        functools.partial(_attn_kernel, tk=tk),
        grid=(b, heads, t // tq),
        in_specs=[
            pl.BlockSpec((1, 1, tq, QK_PAD), lambda i, h, j: (i, h, j, 0)),
            pl.BlockSpec((1, 1, tc, QK_PAD), lambda i, h, j: (i, h, 0, 0)),
            pl.BlockSpec((1, 1, tc, dv), lambda i, h, j: (i, h, 0, 0)),
            pl.BlockSpec((1, 1, t, QK_PAD), lambda i, h, j: (i, h, 0, 0)),
            pl.BlockSpec((1, 1, t, dv), lambda i, h, j: (i, h, 0, 0)),
        ],
        out_specs=pl.BlockSpec((1, tq, dv), lambda i, h, j: (i, j, h)),
        out_shape=jax.ShapeDtypeStruct((b, t, heads * dv), BF16),
        compiler_params=_cparams(("parallel", "parallel", "arbitrary")),
        name="l1_attn",
    )(q, kc, vc, kx, vx)


def _outproj_router_kernel(a_ref, w_ref, x_ref, moda_ref, modf_ref, g_ref, rh_ref, rl_ref,
                           xo_ref, h_ref, sel_ref):
    tm = x_ref.shape[0]
    sub = tm // ROW_SPLIT
    for blk in range(ROW_SPLIT):
        rows = slice(blk * sub, (blk + 1) * sub)
        x = x_ref[rows, :] + moda_ref[0, 2:3, :] * _dot(a_ref[rows, :], w_ref[...])
        xo_ref[rows, :] = x
        h = _rms_mod(x, g_ref[...], modf_ref[0, 0:1, :], modf_ref[0, 1:2, :])
        h_ref[rows, :] = h
        hh = h.astype(BF16)
        hl = (h - hh.astype(F32)).astype(BF16)
        rh = rh_ref[...]
        logits = _dot(hh, rh) + (_dot(hl, rh) + _dot(hh, rl_ref[...]))
        lane = lax.broadcasted_iota(jnp.int32, logits.shape, 1).astype(F32)
        neg = -jnp.inf
        logits = jnp.where(lane < N_EXPERTS, logits, neg)
        v1 = jnp.max(logits, axis=-1, keepdims=True)
        i1 = jnp.min(jnp.where(logits == v1, lane, float(LANE)), axis=-1, keepdims=True)
        rest = jnp.where(lane == i1, neg, logits)
        v2 = jnp.max(rest, axis=-1, keepdims=True)
        i2 = jnp.min(jnp.where(rest == v2, lane, float(LANE)), axis=-1, keepdims=True)
        e2 = jnp.exp(v2 - v1)
        w1 = 1.0 / (1.0 + e2)
        w2 = e2 / (1.0 + e2)
        sel_ref[rows, :] = (jnp.where(lane == 0.0, i1, 0.0) + jnp.where(lane == 1.0, i2, 0.0)
                            + jnp.where(lane == 2.0, w1, 0.0) + jnp.where(lane == 3.0, w2, 0.0))


def _outproj_router(a, w, x2, mod_a, mod_f, mod_map, g, r_hi, r_lo, tm):
    n, d = x2.shape
    row = lambda t: (t, 0)
    const = lambda t: (0, 0)
    return pl.pallas_call(
        _outproj_router_kernel,
        grid=(n // tm,),
        in_specs=[
            pl.BlockSpec((tm, a.shape[1]), row),
            pl.BlockSpec(w.shape, const),
            pl.BlockSpec((tm, d), row),
            pl.BlockSpec((1, 3, d), mod_map),
            pl.BlockSpec((1, 3, d), mod_map),
            pl.BlockSpec((1, d), const),
            pl.BlockSpec(r_hi.shape, const),
            pl.BlockSpec(r_lo.shape, const),
        ],
        out_specs=[pl.BlockSpec((tm, d), row), pl.BlockSpec((tm, d), row), pl.BlockSpec((tm, LANE), row)],
        out_shape=[jax.ShapeDtypeStruct((n, d), F32), jax.ShapeDtypeStruct((n, d), F32),
                   jax.ShapeDtypeStruct((n, LANE), F32)],
        compiler_params=_cparams(("parallel",)),
        name="l1_outproj_router",
    )(a, w, x2, mod_a, mod_f, g, r_hi, r_lo)


def _routing_tables(sel, tm):
    n = sel.shape[0]
    ids = sel[:, 0:2].astype(jnp.int32).reshape(2 * n)
    onehot = (ids[:, None] == jnp.arange(N_EXPERTS, dtype=jnp.int32)[None, :]).astype(jnp.int32)
    csum = jnp.cumsum(onehot, axis=0)
    rank = jnp.sum(csum * onehot, axis=1) - 1
    tiles = (csum[-1] + tm - 1) // tm
    tile_end = jnp.cumsum(tiles)
    start = (tile_end - tiles) * tm
    pos = start[ids] + rank
    n_tiles = (2 * n) // tm + N_EXPERTS
    tile_expert = jnp.sum(jnp.arange(n_tiles, dtype=jnp.int32)[:, None] >= tile_end[None, :], axis=1)
    tile_expert = jnp.minimum(tile_expert, N_EXPERTS - 1).astype(jnp.int32)
    n_pad = n_tiles * tm - 2 * n
    region_start = jnp.concatenate([start + csum[-1], tile_end[-1:] * tm])
    region_size = jnp.concatenate([tiles * tm - csum[-1], n_tiles * tm - tile_end[-1:] * tm])
    region_end = jnp.cumsum(region_size)
    q = jnp.arange(n_pad, dtype=jnp.int32)
    region = jnp.sum(q[:, None] >= region_end[None, :], axis=1)
    pad_slots = region_start[region] + q - (region_end - region_size)[region]
    return (pos.astype(jnp.int32), pad_slots.astype(jnp.int32).reshape(1, n_pad), tile_expert,
            tile_end[-1:].astype(jnp.int32), n_tiles)


def _row_copy(src, src_row, dst, dst_row, sem):
    return pltpu.make_async_copy(src.at[pl.ds(src_row, 1)], dst.at[pl.ds(dst_row, 1)], sem)


def _scatter_kernel(pos_ref, pad_ref, h_ref, hs_ref, zero_ref, sem):
    tm = h_ref.shape[0]
    n_pad = pad_ref.shape[1]

    def drain(n_rows):
        for _ in range(n_rows // tm):
            pltpu.make_async_copy(h_ref, hs_ref.at[pl.ds(0, tm)], sem).wait()

    @pl.when(pl.program_id(0) == 0)
    def _():
        zero_ref[...] = jnp.zeros_like(zero_ref)

        def issue_pad(q, carry):
            _row_copy(zero_ref, 0, hs_ref, pad_ref[0, q], sem).start()
            return carry

        lax.fori_loop(0, n_pad, issue_pad, 0, unroll=8)
        drain(n_pad)

    def issue(r, carry):
        _row_copy(h_ref, r, hs_ref, pos_ref[0, 2 * r], sem).start()
        _row_copy(h_ref, r, hs_ref, pos_ref[0, 2 * r + 1], sem).start(priority=1)
        return carry

    lax.fori_loop(0, tm, issue, 0, unroll=8)
    drain(2 * tm)


def _scatter_rows(pos2, pad_slots, hp, n_slots, tm):
    n, w = hp.shape
    assert pad_slots.shape[1] % tm == 0
    return pl.pallas_call(
        _scatter_kernel,
        grid=(n // tm,),
        in_specs=[
            pl.BlockSpec((None, 1, 2 * tm), lambda t: (t, 0, 0), memory_space=pltpu.SMEM),
            pl.BlockSpec(pad_slots.shape, lambda t: (0, 0), memory_space=pltpu.SMEM),
            pl.BlockSpec((tm, w), lambda t: (t, 0)),
        ],
        out_specs=pl.BlockSpec(memory_space=pl.ANY),
        out_shape=jax.ShapeDtypeStruct((n_slots, w), hp.dtype),
        scratch_shapes=[pltpu.VMEM((8, w), hp.dtype), pltpu.SemaphoreType.DMA(())],
        compiler_params=_cparams(("arbitrary",)),
        name="moe_scatter",
    )(pos2, pad_slots, hp)


def _moe_ffn_kernel(te_ref, nt_ref, hs_ref, w1_ref, w3_ref, w2_ref, y_ref, *, tf):
    del te_ref
    valid = pl.program_id(0) < nt_ref[0]

    @pl.when(valid)
    def _():
        h = hs_ref[...].astype(BF16)
        acc = None
        for c in range(w1_ref.shape[2] // tf):
            cols = slice(c * tf, (c + 1) * tf)
            act = _silu(_dot(h, w1_ref[0, :, cols])) * _dot(h, w3_ref[0, :, cols])
            part = _dot(act.astype(BF16), w2_ref[0, cols, :])
            acc = part if acc is None else acc + part
        y_ref[...] = acc

    @pl.when(jnp.logical_not(valid))
    def _():
        y_ref[...] = jnp.zeros_like(y_ref)


def _moe_ffn(tile_expert, n_tiles_used, hs, w1, w3, w2, tm, tf):
    p, w = hs.shape
    _, d, dff = w1.shape
    expert = lambda j, te, nt: (te[j], 0, 0)
    resident = lambda shape: pl.BlockSpec(shape, expert, pipeline_mode=pl.Buffered(1))
    grid_spec = pltpu.PrefetchScalarGridSpec(
        num_scalar_prefetch=2,
        grid=(p // tm,),
        in_specs=[
            pl.BlockSpec((tm, w), lambda j, te, nt: (j, 0)),
            resident((1, d, dff)), resident((1, d, dff)), resident((1, dff, d)),
        ],
        out_specs=pl.BlockSpec((tm, w), lambda j, te, nt: (j, 0)),
    )
    return pl.pallas_call(
        functools.partial(_moe_ffn_kernel, tf=tf),
        grid_spec=grid_spec,
        out_shape=jax.ShapeDtypeStruct((p, w), F32),
        compiler_params=_cparams(("arbitrary",)),
        name="moe_ffn",
    )(tile_expert, n_tiles_used, hs, w1, w3, w2)


def _combine_kernel(pos_ref, nxt_ref, x_ref, mod_ref, sel_ref, fg_ref, ys_ref, o_ref, buf, sem):
    tm = x_ref.shape[0]
    t = pl.program_id(0)
    slot = t % 2

    def gather(p_ref, s):
        def issue(r, carry):
            _row_copy(ys_ref, p_ref[0, 2 * r], buf.at[s, 0], r, sem.at[s]).start()
            _row_copy(ys_ref, p_ref[0, 2 * r + 1], buf.at[s, 1], r, sem.at[s]).start(priority=1)
            return carry

        lax.fori_loop(0, tm, issue, 0, unroll=8)

    @pl.when(t == 0)
    def _():
        gather(pos_ref, 0)

    @pl.when(t + 1 < pl.num_programs(0))
    def _():
        gather(nxt_ref, 1 - slot)

    for k in range(2):
        pltpu.make_async_copy(ys_ref.at[pl.ds(0, tm)], buf.at[slot, k], sem.at[slot]).wait()
    sel = sel_ref[...]
    y = sel[:, 2:3] * buf[slot, 0] + sel[:, 3:4] * buf[slot, 1]
    o_ref[...] = _rms(x_ref[...] + mod_ref[0, 2:3, :] * y, fg_ref[...])


def _combine(pos2, x2, mod, mod_map, sel, final_g, ys, tm):
    n, d = x2.shape
    w = ys.shape[1]
    steps = n // tm
    row = lambda t: (t, 0)
    return pl.pallas_call(
        _combine_kernel,
        grid=(steps,),
        in_specs=[
            pl.BlockSpec((None, 1, 2 * tm), lambda t: (t, 0, 0), memory_space=pltpu.SMEM),
            pl.BlockSpec((None, 1, 2 * tm), lambda t: (jnp.minimum(t + 1, steps - 1), 0, 0),
                         memory_space=pltpu.SMEM),
            pl.BlockSpec((tm, d), row),
            pl.BlockSpec((1, 3, d), mod_map),
            pl.BlockSpec((tm, LANE), row),
            pl.BlockSpec((1, d), lambda t: (0, 0)),
            pl.BlockSpec(memory_space=pl.ANY),
        ],
        out_specs=pl.BlockSpec((tm, d), row),
        out_shape=jax.ShapeDtypeStruct((n, d), F32),
        scratch_shapes=[pltpu.VMEM((2, 2, tm, w), F32), pltpu.SemaphoreType.DMA((2,))],
        compiler_params=_cparams(("arbitrary",)),
        name="moe_combine",
    )(pos2, pos2, x2, mod, sel, final_g, ys)


def _rope_tables(t):
    freqs = QK_ROPE // 4
    rows = t // GRID_W
    row = jnp.repeat(jnp.arange(rows), GRID_W).astype(F32)
    col = jnp.tile(jnp.arange(GRID_W), rows).astype(F32)
    inv_freq = ROPE_BASE ** (-jnp.arange(freqs, dtype=F32) / freqs)
    ang = jnp.stack([row[:, None] * inv_freq, col[:, None] * inv_freq], axis=1)
    cos = jnp.cos(ang)
    sin = jnp.sin(ang)
    c64 = jnp.stack([cos, cos], axis=2).reshape(t, QK_ROPE)
    s64 = jnp.stack([-sin, sin], axis=2).reshape(t, QK_ROPE)
    pad = jnp.zeros((t, LANE - QK_ROPE), F32)
    return jnp.concatenate([c64, pad], axis=1), jnp.concatenate([s64, pad], axis=1)


def _rope_partner_index():
    freqs = QK_ROPE // 4
    d = jnp.arange(QK_ROPE)
    half = (d // freqs) % 2
    return jnp.where(half == 0, d + freqs, d - freqs)


def _mla_weights(o_w_in, o_w_uq, o_w_ukv):
    heads = o_w_ukv.shape[1] // (QK_NOPE + V_HD)
    pidx = _rope_partner_index()
    pe = o_w_in[:, Q_RANK + KV_RANK:]
    win = jnp.concatenate([o_w_in, pe[:, pidx]], axis=1).astype(BF16)
    uq = o_w_uq.reshape(Q_RANK, heads, QK_NOPE + QK_ROPE)
    nope, rope = uq[..., :QK_NOPE], uq[..., QK_NOPE:]
    wa = jnp.concatenate([nope, rope, rope[..., pidx]], axis=-1).reshape(Q_RANK, heads * QK_PAD).astype(BF16)
    return win, wa, o_w_ukv.astype(BF16)


def kernel(x, c, ctx, c_ctx, w_ada, b_ada, norm_g, e_w_in, e_conv_w, e_conv_b, e_ln_a_g, e_ln_a_b, e_ln_v_g, e_ln_v_b, e_w_s, e_b_s, e_w_out, e_ffn_w1, e_ffn_w3, e_ffn_w2, o_w_in, o_g_q, o_g_kv, o_w_uq, o_w_ukv, o_w_o, o_router, o_exp_w1, o_exp_w3, o_exp_w2, final_g):
    b, t, d = x.shape
    tc = ctx.shape[1]
    n, nc = b * t, b * tc
    depth = w_ada.shape[0]
    assert depth == 2 and t % GRID_W == 0 and t % CHUNK == 0 and tc % CHUNK == 0

    tm_x = min(512, t)
    tm_c = min(256, tc)

    rows = 16
    cc = jnp.concatenate([c, c_ctx[None], jnp.zeros((rows - b - 1, d), F32)], axis=0)
    mods = _ada(cc, w_ada, b_ada).reshape(depth, rows, 6, d)
    xmap = lambda tm: (lambda i: (i // (t // tm), 0, 0))
    cmap = lambda i: (b, 0, 0)

    x2 = x.reshape(n, d)
    c2 = ctx.reshape(nc, d)
    row1 = lambda v: v.reshape(1, -1)

    w_in = e_w_in[0].astype(BF16)
    w_out = e_w_out[0].astype(BF16)
    w_s = e_w_s[0].astype(BF16)
    ch = e_conv_w.shape[-1]
    bias_map = jnp.repeat(e_b_s[0].T, ch // B_HEADS, axis=1)
    conv_w = jnp.concatenate([e_conv_w[0], jnp.zeros((1, ch), F32)], axis=0)
    f1 = e_ffn_w1[0].astype(BF16)
    f3 = e_ffn_w3[0].astype(BF16)
    f2 = e_ffn_w2[0].astype(BF16)
    dff = f1.shape[-1]
    tf = 2 * LANE
    assert dff % tf == 0

    def layer0(s2, seq, tm, mod_a, mod_f, mod_map, tm_f):
        a, u, v = _inproj(s2, mod_a, mod_map(tm), row1(norm_g[0, 0]), w_in,
                          row1(e_ln_v_g[0]), row1(e_ln_v_b[0]), tm)
        s2 = _mixer(a, u, v, s2, mod_a, mod_map(tm), conv_w, row1(e_conv_b[0]),
                    row1(e_ln_a_g[0]), row1(e_ln_a_b[0]), w_s, bias_map, w_out, tm, seq)
        return _ffn(s2, mod_f, mod_map(tm_f), row1(norm_g[0, 1]), f1, f3, f2, tm_f, tf)

    m0a, m0f = mods[0, :, 0:3], mods[0, :, 3:6]
    x2 = layer0(x2, t, tm_x, m0a, m0f, xmap, tm_x)
    c2 = layer0(c2, tc, tm_c, m0a, m0f, lambda tm: cmap, tm_c)

    m1a, m1f = mods[1, :, 0:3], mods[1, :, 3:6]
    win, wa, wukv = _mla_weights(o_w_in[0], o_w_uq[0], o_w_ukv[0])
    cos, sin = _rope_tables(t)
    ones = jnp.concatenate([jnp.ones((tc, QK_ROPE), F32), jnp.zeros((tc, LANE - QK_ROPE), F32)], axis=1)
    zeros = jnp.zeros((tc, LANE), F32)
    g10 = row1(norm_g[1, 0])
    proj = functools.partial(_mla_proj, g=g10, win=win, gq=row1(o_g_q[0]),
                             gkv=row1(o_g_kv[0]), wa=wa, wukv=wukv)
    q_x, k_x, v_x = proj(x2.reshape(b, t, d), m1a, xmap(tm_x), cos=cos, sin=sin, tm=tm_x)
    _, k_c, v_c = proj(c2.reshape(b, tc, d), m1a, cmap, cos=ones, sin=zeros, tm=tm_c)
    att = _attention(q_x, k_c, v_c, k_x, v_x, tq=min(4096, t), tk=min(256, t))

    r_pad = jnp.concatenate([o_router[0], jnp.zeros((d, LANE - N_EXPERTS), F32)], axis=1)
    r_hi = r_pad.astype(BF16)
    r_lo = (r_pad - r_hi.astype(F32)).astype(BF16)
    x2, hp, sel = _outproj_router(att.reshape(n, -1), o_w_o[0].astype(BF16), x2, m1a, m1f, xmap(tm_x),
                                  row1(norm_g[1, 1]), r_hi, r_lo, tm_x)
    tm_e = 512
    pos, pad_slots, tile_expert, n_used, n_tiles = _routing_tables(sel, tm_e)
    tm_s = min(2048, t)
    hs = _scatter_rows(pos.reshape(n // tm_s, 1, 2 * tm_s), pad_slots, hp, n_tiles * tm_e, tm_s)
    dexp = o_exp_w1.shape[-1]
    tf_e = 2 * LANE
    assert dexp % tf_e == 0
    ys = _moe_ffn(tile_expert, n_used, hs, o_exp_w1[0].astype(BF16), o_exp_w3[0].astype(BF16),
                  o_exp_w2[0].astype(BF16), tm_e, tf_e)
    tm_k = min(512, t)
    out = _combine(pos.reshape(n // tm_k, 1, 2 * tm_k), x2, m1f, xmap(tm_k), sel, row1(final_g), ys, tm_k)
    return out.reshape(b, t, d)
```

```python
import functools

import jax
import jax.numpy as jnp
from jax import lax
from jax.experimental import pallas as pl
from jax.experimental.pallas import tpu as pltpu

F32 = jnp.float32
BF16 = jnp.bfloat16

EPS = 1e-6
GRID_W = 64
CONV_K = 31
CHUNK = 128
B_HEADS = 8
QK_NOPE = 128
QK_ROPE = 64
V_HD = 128
Q_RANK = 384
KV_RANK = 256
ROPE_BASE = 10000.0
N_EXPERTS = 8
ATTN_SCALE = (QK_NOPE + QK_ROPE) ** -0.5
LOG2_E = 1.4426950408889634
Q_SCALE = ATTN_SCALE * LOG2_E

LANE = 128
SUBLANE = 8
HALO = 16
CONV_ROWS = 64
QK_PAD = 256
ROW_SPLIT = 2
VMEM_LIMIT = 52 * 1024 * 1024


def _cparams(sem):
    return pltpu.CompilerParams(dimension_semantics=sem, vmem_limit_bytes=VMEM_LIMIT)


def _silu(x):
    return x * jax.nn.sigmoid(x)


def _rms_mod(x, g, shift, scale):
    y = x * lax.rsqrt(jnp.mean(x * x, axis=-1, keepdims=True) + EPS) * g
    return y * (1.0 + scale) + shift


def _rms(x, g):
    return x * lax.rsqrt(jnp.mean(x * x, axis=-1, keepdims=True) + EPS) * g


def _dot(a, b):
    return jnp.dot(a, b, preferred_element_type=F32)


def _ada_kernel(cc_ref, w_ref, b_ref, o_ref):
    s = _silu(cc_ref[...]).astype(BF16)
    o_ref[0] = _dot(s, w_ref[0].astype(BF16)) + b_ref[0]


def _ada(cc, w_ada, b_ada):
    depth, d, d6 = w_ada.shape
    rows = cc.shape[0]
    tn = 1536
    return pl.pallas_call(
        _ada_kernel,
        grid=(depth, d6 // tn),
        in_specs=[
            pl.BlockSpec((rows, d), lambda i, j: (0, 0)),
            pl.BlockSpec((1, d, tn), lambda i, j: (i, 0, j)),
            pl.BlockSpec((1, 1, tn), lambda i, j: (i, 0, j)),
        ],
        out_specs=pl.BlockSpec((1, rows, tn), lambda i, j: (i, 0, j)),
        out_shape=jax.ShapeDtypeStruct((depth, rows, d6), F32),
        compiler_params=_cparams(("parallel", "parallel")),
        name="ada",
    )(cc, w_ada, b_ada.reshape(depth, 1, d6))


def _inproj_kernel(x_ref, mod_ref, g_ref, w_ref, lng_ref, lnb_ref, a_ref, u_ref, v_ref):
    ch = a_ref.shape[1]
    h = _rms_mod(x_ref[...], g_ref[...], mod_ref[0, 0:1, :], mod_ref[0, 1:2, :]).astype(BF16)
    z = _dot(h, w_ref[...])
    a = z[:, 0:ch] * jax.nn.sigmoid(z[:, ch:2 * ch])
    u = jax.nn.gelu(z[:, 2 * ch:3 * ch])
    v = jax.nn.gelu(z[:, 3 * ch:4 * ch])
    mu = jnp.mean(v, axis=-1, keepdims=True)
    vc = v - mu
    var = jnp.mean(vc * vc, axis=-1, keepdims=True)
    v = vc * lax.rsqrt(var + EPS) * lng_ref[...] + lnb_ref[...]
    a_ref[...] = a.astype(BF16)
    u_ref[...] = u.astype(BF16)
    v_ref[...] = v.astype(BF16)


def _inproj(x2, mod, mod_map, g, w_in, ln_g, ln_b, tm):
    n, d = x2.shape
    ch = w_in.shape[1] // 4
    row = lambda t: (t, 0)
    const = lambda t: (0, 0)
    out = jax.ShapeDtypeStruct((n, ch), BF16)
    return pl.pallas_call(
        _inproj_kernel,
        grid=(n // tm,),
        in_specs=[
            pl.BlockSpec((tm, d), row),
            pl.BlockSpec((1, 3, d), mod_map),
            pl.BlockSpec((1, d), const),
            pl.BlockSpec(w_in.shape, const),
            pl.BlockSpec((1, ch), const),
            pl.BlockSpec((1, ch), const),
        ],
        out_specs=[pl.BlockSpec((tm, ch), row)] * 3,
        out_shape=[out, out, out],
        compiler_params=_cparams(("parallel",)),
        name="l0_inproj",
    )(x2, mod, g, w_in, ln_g, ln_b)


def _mixer_kernel(a_ref, ap_ref, an_ref, u_ref, v_ref, x_ref, mod_ref, cw_ref, cb_ref,
                  lag_ref, lab_ref, ws_ref, bs_ref, wo_ref, o_ref,
                  ext_ref, sh_ref, conv_ref, cat_ref, *, tiles_per_seq):
    tm, ch = a_ref.shape
    t = pl.program_id(0)
    first = (t % tiles_per_seq) == 0
    last = (t % tiles_per_seq) == tiles_per_seq - 1

    ext_ref[0:HALO, :] = jnp.where(first, 0.0, ap_ref[...].astype(F32))
    ext_ref[HALO:HALO + tm, :] = a_ref[...].astype(F32)
    ext_ref[HALO + tm:, :] = jnp.where(last, 0.0, an_ref[...].astype(F32))
    span = tm + 24
    for r in range(1, 8):
        sh_ref[r - 1] = ext_ref[r:r + span, :]

    off0 = HALO - CONV_K // 2
    cb = cb_ref[...]

    def conv_block(rb, carry):
        base = pl.multiple_of(rb * CONV_ROWS, CONV_ROWS)
        acc = jnp.broadcast_to(cb, (CONV_ROWS, ch))
        for k in range(CONV_K):
            q, r = divmod(off0 + k, 8)
            start = pl.multiple_of(base + 8 * q, 8)
            if r == 0:
                win = ext_ref[pl.ds(start, CONV_ROWS), :]
            else:
                win = sh_ref[r - 1, pl.ds(start, CONV_ROWS), :]
            acc = acc + cw_ref[k:k + 1, :] * win
        conv_ref[pl.ds(base, CONV_ROWS), :] = acc
        return carry

    lax.fori_loop(0, tm // CONV_ROWS, conv_block, 0)

    acc = conv_ref[...]
    mu = jnp.mean(acc, axis=-1, keepdims=True)
    c = acc - mu
    var = jnp.mean(c * c, axis=-1, keepdims=True)
    y = c * lax.rsqrt(var + EPS) * lag_ref[...] + lab_ref[...]
    cat_ref[:, 0:ch] = _silu(y).astype(BF16)

    lane = lax.broadcasted_iota(jnp.int32, (CHUNK, LANE), 1)
    lo_half = lane < (LANE // 2)
    for c in range(tm // CHUNK):
        rows = slice(c * CHUNK, (c + 1) * CHUNK)
        for p in range(ch // LANE):
            cols = slice(p * LANE, (p + 1) * LANE)
            vv = v_ref[rows, cols]
            s0 = _dot(ws_ref[2 * p], vv)
            s1 = _dot(ws_ref[2 * p + 1], vv)
            sv = jnp.where(lo_half, s0, s1) + bs_ref[:, cols]
            bo = u_ref[rows, cols].astype(F32) * sv
            cat_ref[rows, ch + p * LANE:ch + (p + 1) * LANE] = bo.astype(BF16)

    y = _dot(cat_ref[...], wo_ref[...])
    o_ref[...] = x_ref[...] + mod_ref[0, 2:3, :] * y


def _mixer(a, u, v, x2, mod, mod_map, conv_w, conv_b, la_g, la_b, w_s, bias_map, w_out, tm, seq):
    n, d = x2.shape
    ch = a.shape[1]
    hb = tm // HALO
    nhb = n // HALO
    row = lambda t: (t, 0)
    const = lambda t: (0, 0)
    kern = functools.partial(_mixer_kernel, tiles_per_seq=seq // tm)
    return pl.pallas_call(
        kern,
        grid=(n // tm,),
        in_specs=[
            pl.BlockSpec((tm, ch), row),
            pl.BlockSpec((HALO, ch), lambda t: (jnp.maximum(t * hb - 1, 0), 0)),
            pl.BlockSpec((HALO, ch), lambda t: (jnp.minimum((t + 1) * hb, nhb - 1), 0)),
            pl.BlockSpec((tm, ch), row),
            pl.BlockSpec((tm, ch), row),
            pl.BlockSpec((tm, d), row),
            pl.BlockSpec((1, 3, d), mod_map),
            pl.BlockSpec(conv_w.shape, const),
            pl.BlockSpec((1, ch), const),
            pl.BlockSpec((1, ch), const),
            pl.BlockSpec((1, ch), const),
            pl.BlockSpec(w_s.shape, lambda t: (0, 0, 0)),
            pl.BlockSpec(bias_map.shape, const),
            pl.BlockSpec(w_out.shape, const),
        ],
        out_specs=pl.BlockSpec((tm, d), row),
        out_shape=jax.ShapeDtypeStruct((n, d), F32),
        scratch_shapes=[
            pltpu.VMEM((tm + 2 * HALO, ch), F32),
            pltpu.VMEM((7, tm + 24, ch), F32),
            pltpu.VMEM((tm, ch), F32),
            pltpu.VMEM((tm, 2 * ch), BF16),
        ],
        compiler_params=_cparams(("parallel",)),
        name="l0_mixer",
    )(a, a, a, u, v, x2, mod, conv_w, conv_b, la_g, la_b, w_s, bias_map, w_out)


def _ffn_kernel(x_ref, mod_ref, g_ref, w1_ref, w3_ref, w2_ref, o_ref, *, tf):
    x = x_ref[...]
    h = _rms_mod(x, g_ref[...], mod_ref[0, 0:1, :], mod_ref[0, 1:2, :]).astype(BF16)
    acc = None
    for c in range(w1_ref.shape[1] // tf):
        cols = slice(c * tf, (c + 1) * tf)
        act = _silu(_dot(h, w1_ref[:, cols])) * _dot(h, w3_ref[:, cols])
        part = _dot(act.astype(BF16), w2_ref[cols, :])
        acc = part if acc is None else acc + part
    o_ref[...] = x + mod_ref[0, 2:3, :] * acc


def _ffn(x2, mod, mod_map, g, w1, w3, w2, tm, tf):
    n, d = x2.shape
    row = lambda t: (t, 0)
    resident = lambda w: pl.BlockSpec(w.shape, lambda t: (0, 0), pipeline_mode=pl.Buffered(1))
    return pl.pallas_call(
        functools.partial(_ffn_kernel, tf=tf),
        grid=(n // tm,),
        in_specs=[
            pl.BlockSpec((tm, d), row),
            pl.BlockSpec((1, 3, d), mod_map),
            pl.BlockSpec((1, d), lambda t: (0, 0)),
            resident(w1), resident(w3), resident(w2),
        ],
        out_specs=pl.BlockSpec((tm, d), row),
        out_shape=jax.ShapeDtypeStruct((n, d), F32),
        compiler_params=_cparams(("parallel",)),
        name="ffn",
    )(x2, mod, g, w1, w3, w2)


def _rotate(t, cos, sin):
    return t * cos + pltpu.roll(t, LANE // 2, 1) * sin


def _mla_proj_kernel(x_ref, mod_ref, g_ref, win_ref, gq_ref, gkv_ref, wa_ref, wukv_ref, cos_ref, sin_ref,
                     q_ref, k_ref, v_ref):
    heads = q_ref.shape[1]
    h = _rms_mod(x_ref[...], g_ref[...], mod_ref[0, 0:1, :], mod_ref[0, 1:2, :]).astype(BF16)
    cos = cos_ref[...]
    sin = sin_ref[...]
    z = _dot(h, win_ref[...])
    kr = _rotate(z[:, Q_RANK + KV_RANK:], cos, sin).astype(BF16)
    qn = _rms(z[:, :Q_RANK], gq_ref[...]).astype(BF16)
    qa = _dot(qn, wa_ref[...])
    kvn = _rms(z[:, Q_RANK:Q_RANK + KV_RANK], gkv_ref[...]).astype(BF16)
    kv = _dot(kvn, wukv_ref[...])
    for hd in range(heads):
        q_ref[0, hd, :, 0:LANE] = (qa[:, hd * QK_PAD:hd * QK_PAD + LANE] * Q_SCALE).astype(BF16)
        qr = _rotate(qa[:, hd * QK_PAD + LANE:(hd + 1) * QK_PAD], cos, sin)
        q_ref[0, hd, :, LANE:QK_PAD] = (qr * Q_SCALE).astype(BF16)
        k_ref[0, hd, :, 0:LANE] = kv[:, hd * 2 * LANE:hd * 2 * LANE + LANE].astype(BF16)
        k_ref[0, hd, :, LANE:QK_PAD] = kr
        v_ref[0, hd, :, :] = kv[:, hd * 2 * LANE + LANE:(hd + 1) * 2 * LANE].astype(BF16)


def _mla_proj(x3, mod, mod_map, g, win, gq, gkv, wa, wukv, cos, sin, tm):
    b, t, d = x3.shape
    heads = wukv.shape[1] // (2 * LANE)
    tps = t // tm
    row = lambda i: (i // tps, i % tps, 0)
    const = lambda i: (0, 0)
    tab = lambda i: (i % tps, 0)
    hrow = lambda i: (i // tps, 0, i % tps, 0)
    return pl.pallas_call(
        _mla_proj_kernel,
        grid=(b * tps,),
        in_specs=[
            pl.BlockSpec((None, tm, d), row),
            pl.BlockSpec((1, 3, d), mod_map),
            pl.BlockSpec((1, d), const),
            pl.BlockSpec(win.shape, const),
            pl.BlockSpec(gq.shape, const),
            pl.BlockSpec(gkv.shape, const),
            pl.BlockSpec(wa.shape, const),
            pl.BlockSpec(wukv.shape, const),
            pl.BlockSpec((tm, LANE), tab),
            pl.BlockSpec((tm, LANE), tab),
        ],
        out_specs=[
            pl.BlockSpec((1, heads, tm, QK_PAD), hrow),
            pl.BlockSpec((1, heads, tm, QK_PAD), hrow),
            pl.BlockSpec((1, heads, tm, LANE), hrow),
        ],
        out_shape=[
            jax.ShapeDtypeStruct((b, heads, t, QK_PAD), BF16),
            jax.ShapeDtypeStruct((b, heads, t, QK_PAD), BF16),
            jax.ShapeDtypeStruct((b, heads, t, LANE), BF16),
        ],
        compiler_params=_cparams(("parallel",)),
        name="l1_proj",
    )(x3, mod, g, win, gq, gkv, wa, wukv, cos, sin)


def _attn_kernel(q_ref, kc_ref, vc_ref, kx_ref, vx_ref, o_ref, *, tk):
    q = q_ref[0, 0]
    tq = q.shape[0]

    def step(k, v, m, l, acc):
        s = lax.dot_general(q, k, (((1,), (1,)), ((), ())), preferred_element_type=F32)
        m_new = jnp.maximum(m, jnp.max(s, axis=-1, keepdims=True))
        alpha = jnp.exp2(m - m_new)
        p = jnp.exp2(s - m_new)
        l = alpha * l + jnp.sum(p, axis=-1, keepdims=True)
        acc = alpha * acc + _dot(p.astype(BF16), v)
        return m_new, l, acc

    m = jnp.full((tq, 1), -jnp.inf, F32)
    l = jnp.zeros((tq, 1), F32)
    acc = jnp.zeros((tq, vc_ref.shape[-1]), F32)
    m, l, acc = step(kc_ref[0, 0], vc_ref[0, 0], m, l, acc)
    for c in range(kx_ref.shape[2] // tk):
        m, l, acc = step(kx_ref[0, 0, c * tk:(c + 1) * tk, :], vx_ref[0, 0, c * tk:(c + 1) * tk, :],
                         m, l, acc)
    o_ref[0] = (acc / l).astype(BF16)


def _attention(q, kc, vc, kx, vx, tq, tk):
    b, heads, t, _ = q.shape
    tc = kc.shape[2]
    dv = vx.shape[-1]
    return pl.pallas_call(
        functools.partial(_attn_kernel, tk=tk),
        grid=(b, heads, t // tq),
        in_specs=[
            pl.BlockSpec((1, 1, tq, QK_PAD), lambda i, h, j: (i, h, j, 0)),
            pl.BlockSpec((1, 1, tc, QK_PAD), lambda i, h, j: (i, h, 0, 0)),
            pl.BlockSpec((1, 1, tc, dv), lambda i, h, j: (i, h, 0, 0)),
            pl.BlockSpec((1, 1, t, QK_PAD), lambda i, h, j: (i, h, 0, 0)),
            pl.BlockSpec((1, 1, t, dv), lambda i, h, j: (i, h, 0, 0)),
        ],
        out_specs=pl.BlockSpec((1, tq, dv), lambda i, h, j: (i, j, h)),
        out_shape=jax.ShapeDtypeStruct((b, t, heads * dv), BF16),
        compiler_params=_cparams(("parallel", "parallel", "arbitrary")),
        name="l1_attn",
    )(q, kc, vc, kx, vx)


def _outproj_router_kernel(a_ref, w_ref, x_ref, moda_ref, modf_ref, g_ref, rh_ref, rl_ref,
                           xo_ref, h_ref, sel_ref):
    tm = x_ref.shape[0]
    sub = tm // ROW_SPLIT
    for blk in range(ROW_SPLIT):
        rows = slice(blk * sub, (blk + 1) * sub)
        x = x_ref[rows, :] + moda_ref[0, 2:3, :] * _dot(a_ref[rows, :], w_ref[...])
        xo_ref[rows, :] = x
        h = _rms_mod(x, g_ref[...], modf_ref[0, 0:1, :], modf_ref[0, 1:2, :])
        _to_row_tiles(h_ref, blk * sub, h)
        hh = h.astype(BF16)
        hl = (h - hh.astype(F32)).astype(BF16)
        rh = rh_ref[...]
        logits = _dot(hh, rh) + (_dot(hl, rh) + _dot(hh, rl_ref[...]))
        lane = lax.broadcasted_iota(jnp.int32, logits.shape, 1).astype(F32)
        neg = -jnp.inf
        logits = jnp.where(lane < N_EXPERTS, logits, neg)
        v1 = jnp.max(logits, axis=-1, keepdims=True)
        i1 = jnp.min(jnp.where(logits == v1, lane, float(LANE)), axis=-1, keepdims=True)
        rest = jnp.where(lane == i1, neg, logits)
        v2 = jnp.max(rest, axis=-1, keepdims=True)
        i2 = jnp.min(jnp.where(rest == v2, lane, float(LANE)), axis=-1, keepdims=True)
        e2 = jnp.exp(v2 - v1)
        w1 = 1.0 / (1.0 + e2)
        w2 = e2 / (1.0 + e2)
        sel_ref[rows, :] = (jnp.where(lane == 0.0, i1, 0.0) + jnp.where(lane == 1.0, i2, 0.0)
                            + jnp.where(lane == 2.0, w1, 0.0) + jnp.where(lane == 3.0, w2, 0.0))


def _outproj_router(a, w, x2, mod_a, mod_f, mod_map, g, r_hi, r_lo, tm):
    n, d = x2.shape
    row = lambda t: (t, 0)
    const = lambda t: (0, 0)
    return pl.pallas_call(
        _outproj_router_kernel,
        grid=(n // tm,),
        in_specs=[
            pl.BlockSpec((tm, a.shape[1]), row),
            pl.BlockSpec(w.shape, const),
            pl.BlockSpec((tm, d), row),
            pl.BlockSpec((1, 3, d), mod_map),
            pl.BlockSpec((1, 3, d), mod_map),
            pl.BlockSpec((1, d), const),
            pl.BlockSpec(r_hi.shape, const),
            pl.BlockSpec(r_lo.shape, const),
        ],
        out_specs=[pl.BlockSpec((tm, d), row), pl.BlockSpec((tm * SUBLANE, LANE), row),
                   pl.BlockSpec((tm, LANE), row)],
        out_shape=[jax.ShapeDtypeStruct((n, d), F32), jax.ShapeDtypeStruct((n * SUBLANE, LANE), F32),
                   jax.ShapeDtypeStruct((n, LANE), F32)],
        compiler_params=_cparams(("parallel",)),
        name="l1_outproj_router",
    )(a, w, x2, mod_a, mod_f, g, r_hi, r_lo)


def _routing_tables(sel, tm):
    n = sel.shape[0]
    ids = sel[:, 0:2].astype(jnp.int32).reshape(2 * n)
    onehot = (ids[:, None] == jnp.arange(N_EXPERTS, dtype=jnp.int32)[None, :]).astype(jnp.int32)
    csum = jnp.cumsum(onehot, axis=0)
    rank = jnp.sum(csum * onehot, axis=1) - 1
    tiles = (csum[-1] + tm - 1) // tm
    tile_end = jnp.cumsum(tiles)
    start = (tile_end - tiles) * tm
    pos = start[ids] + rank
    n_tiles = (2 * n) // tm + N_EXPERTS
    tile_expert = jnp.sum(jnp.arange(n_tiles, dtype=jnp.int32)[:, None] >= tile_end[None, :], axis=1)
    tile_expert = jnp.minimum(tile_expert, N_EXPERTS - 1).astype(jnp.int32)
    n_pad = n_tiles * tm - 2 * n
    region_start = jnp.concatenate([start + csum[-1], tile_end[-1:] * tm])
    region_size = jnp.concatenate([tiles * tm - csum[-1], n_tiles * tm - tile_end[-1:] * tm])
    region_end = jnp.cumsum(region_size)
    q = jnp.arange(n_pad, dtype=jnp.int32)
    region = jnp.sum(q[:, None] >= region_end[None, :], axis=1)
    pad_slots = region_start[region] + q - (region_end - region_size)[region]
    return (pos.astype(jnp.int32), pad_slots.astype(jnp.int32).reshape(1, n_pad), tile_expert,
            tile_end[-1:].astype(jnp.int32), n_tiles)


def _to_row_tiles(ref, first_row, x):
    m = x.shape[0]
    for s in range(SUBLANE):
        ref[pl.ds(first_row * SUBLANE + s, m, stride=SUBLANE), :] = x[:, s * LANE:(s + 1) * LANE]


def _from_row_tiles(ref, m):
    return jnp.concatenate([ref[pl.ds(s, m, stride=SUBLANE), :] for s in range(SUBLANE)], axis=1)


def _row_copy(src, src_row, dst, dst_row, sem):
    def start(row):
        return row * SUBLANE if isinstance(row, int) else pl.multiple_of(row * SUBLANE, SUBLANE)

    return pltpu.make_async_copy(src.at[pl.ds(start(src_row), SUBLANE)],
                                 dst.at[pl.ds(start(dst_row), SUBLANE)], sem)


def _scatter_kernel(pos_ref, pad_ref, h_ref, hs_ref, zero_ref, sem):
    tm = h_ref.shape[0] // SUBLANE
    n_pad = pad_ref.shape[1]

    def drain(n_rows):
        for _ in range(n_rows // tm):
            pltpu.make_async_copy(h_ref, hs_ref.at[pl.ds(0, tm * SUBLANE)], sem).wait()

    @pl.when(pl.program_id(0) == 0)
    def _():
        zero_ref[...] = jnp.zeros_like(zero_ref)

        def issue_pad(q, carry):
            _row_copy(zero_ref, 0, hs_ref, pad_ref[0, q], sem).start()
            return carry

        lax.fori_loop(0, n_pad, issue_pad, 0, unroll=8)
        drain(n_pad)

    def issue(r, carry):
        _row_copy(h_ref, r, hs_ref, pos_ref[0, 2 * r], sem).start()
        _row_copy(h_ref, r, hs_ref, pos_ref[0, 2 * r + 1], sem).start(priority=1)
        return carry

    lax.fori_loop(0, tm, issue, 0, unroll=8)
    drain(2 * tm)


def _scatter_rows(pos2, pad_slots, hp, n_slots, tm):
    n = hp.shape[0] // SUBLANE
    assert pad_slots.shape[1] % tm == 0
    return pl.pallas_call(
        _scatter_kernel,
        grid=(n // tm,),
        in_specs=[
            pl.BlockSpec((None, 1, 2 * tm), lambda t: (t, 0, 0), memory_space=pltpu.SMEM),
            pl.BlockSpec(pad_slots.shape, lambda t: (0, 0), memory_space=pltpu.SMEM),
            pl.BlockSpec((tm * SUBLANE, LANE), lambda t: (t, 0)),
        ],
        out_specs=pl.BlockSpec(memory_space=pl.ANY),
        out_shape=jax.ShapeDtypeStruct((n_slots * SUBLANE, LANE), hp.dtype),
        scratch_shapes=[pltpu.VMEM((SUBLANE, LANE), hp.dtype), pltpu.SemaphoreType.DMA(())],
        compiler_params=_cparams(("arbitrary",)),
        name="moe_scatter",
    )(pos2, pad_slots, hp)


def _moe_ffn_kernel(te_ref, nt_ref, hs_ref, w1_ref, w3_ref, w2_ref, y_ref, *, tf):
    del te_ref
    tm = hs_ref.shape[0] // SUBLANE
    valid = pl.program_id(0) < nt_ref[0]

    @pl.when(valid)
    def _():
        h = _from_row_tiles(hs_ref, tm).astype(BF16)
        acc = None
        for c in range(w1_ref.shape[2] // tf):
            cols = slice(c * tf, (c + 1) * tf)
            act = _silu(_dot(h, w1_ref[0, :, cols])) * _dot(h, w3_ref[0, :, cols])
            part = _dot(act.astype(BF16), w2_ref[0, cols, :])
            acc = part if acc is None else acc + part
        _to_row_tiles(y_ref, 0, acc)

    @pl.when(jnp.logical_not(valid))
    def _():
        y_ref[...] = jnp.zeros_like(y_ref)


def _moe_ffn(tile_expert, n_tiles_used, hs, w1, w3, w2, tm, tf):
    p = hs.shape[0] // SUBLANE
    _, d, dff = w1.shape
    expert = lambda j, te, nt: (te[j], 0, 0)
    resident = lambda shape: pl.BlockSpec(shape, expert, pipeline_mode=pl.Buffered(1))
    grid_spec = pltpu.PrefetchScalarGridSpec(
        num_scalar_prefetch=2,
        grid=(p // tm,),
        in_specs=[
            pl.BlockSpec((tm * SUBLANE, LANE), lambda j, te, nt: (j, 0)),
            resident((1, d, dff)), resident((1, d, dff)), resident((1, dff, d)),
        ],
        out_specs=pl.BlockSpec((tm * SUBLANE, LANE), lambda j, te, nt: (j, 0)),
    )
    return pl.pallas_call(
        functools.partial(_moe_ffn_kernel, tf=tf),
        grid_spec=grid_spec,
        out_shape=jax.ShapeDtypeStruct(hs.shape, F32),
        compiler_params=_cparams(("arbitrary",)),
        name="moe_ffn",
    )(tile_expert, n_tiles_used, hs, w1, w3, w2)


def _combine_kernel(pos_ref, nxt_ref, x_ref, mod_ref, sel_ref, fg_ref, ys_ref, o_ref, buf, sem):
    tm = x_ref.shape[0]
    t = pl.program_id(0)
    slot = t % 2

    def gather(p_ref, s):
        def issue(r, carry):
            _row_copy(ys_ref, p_ref[0, 2 * r], buf.at[s, 0], r, sem.at[s]).start()
            _row_copy(ys_ref, p_ref[0, 2 * r + 1], buf.at[s, 1], r, sem.at[s]).start(priority=1)
            return carry

        lax.fori_loop(0, tm, issue, 0, unroll=8)

    @pl.when(t == 0)
    def _():
        gather(pos_ref, 0)

    @pl.when(t + 1 < pl.num_programs(0))
    def _():
        gather(nxt_ref, 1 - slot)

    for k in range(2):
        pltpu.make_async_copy(ys_ref.at[pl.ds(0, tm * SUBLANE)], buf.at[slot, k], sem.at[slot]).wait()
    sel = sel_ref[...]
    y = (sel[:, 2:3] * _from_row_tiles(buf.at[slot, 0], tm)
         + sel[:, 3:4] * _from_row_tiles(buf.at[slot, 1], tm))
    o_ref[...] = _rms(x_ref[...] + mod_ref[0, 2:3, :] * y, fg_ref[...])


def _combine(pos2, x2, mod, mod_map, sel, final_g, ys, tm):
    n, d = x2.shape
    steps = n // tm
    row = lambda t: (t, 0)
    return pl.pallas_call(
        _combine_kernel,
        grid=(steps,),
        in_specs=[
            pl.BlockSpec((None, 1, 2 * tm), lambda t: (t, 0, 0), memory_space=pltpu.SMEM),
            pl.BlockSpec((None, 1, 2 * tm), lambda t: (jnp.minimum(t + 1, steps - 1), 0, 0),
                         memory_space=pltpu.SMEM),
            pl.BlockSpec((tm, d), row),
            pl.BlockSpec((1, 3, d), mod_map),
            pl.BlockSpec((tm, LANE), row),
            pl.BlockSpec((1, d), lambda t: (0, 0)),
            pl.BlockSpec(memory_space=pl.ANY),
        ],
        out_specs=pl.BlockSpec((tm, d), row),
        out_shape=jax.ShapeDtypeStruct((n, d), F32),
        scratch_shapes=[pltpu.VMEM((2, 2, tm * SUBLANE, LANE), F32), pltpu.SemaphoreType.DMA((2,))],
        compiler_params=_cparams(("arbitrary",)),
        name="moe_combine",
    )(pos2, pos2, x2, mod, sel, final_g, ys)


def _rope_tables(t):
    freqs = QK_ROPE // 4
    rows = t // GRID_W
    row = jnp.repeat(jnp.arange(rows), GRID_W).astype(F32)
    col = jnp.tile(jnp.arange(GRID_W), rows).astype(F32)
    inv_freq = ROPE_BASE ** (-jnp.arange(freqs, dtype=F32) / freqs)
    ang = jnp.stack([row[:, None] * inv_freq, col[:, None] * inv_freq], axis=1)
    cos = jnp.cos(ang)
    sin = jnp.sin(ang)
    c64 = jnp.stack([cos, cos], axis=2).reshape(t, QK_ROPE)
    s64 = jnp.stack([-sin, sin], axis=2).reshape(t, QK_ROPE)
    pad = jnp.zeros((t, LANE - QK_ROPE), F32)
    return jnp.concatenate([c64, pad], axis=1), jnp.concatenate([s64, pad], axis=1)


def _rope_partner_index():
    freqs = QK_ROPE // 4
    d = jnp.arange(QK_ROPE)
    half = (d // freqs) % 2
    return jnp.where(half == 0, d + freqs, d - freqs)


def _mla_weights(o_w_in, o_w_uq, o_w_ukv):
    heads = o_w_ukv.shape[1] // (QK_NOPE + V_HD)
    pidx = _rope_partner_index()
    pe = o_w_in[:, Q_RANK + KV_RANK:]
    win = jnp.concatenate([o_w_in, pe[:, pidx]], axis=1).astype(BF16)
    uq = o_w_uq.reshape(Q_RANK, heads, QK_NOPE + QK_ROPE)
    nope, rope = uq[..., :QK_NOPE], uq[..., QK_NOPE:]
    wa = jnp.concatenate([nope, rope, rope[..., pidx]], axis=-1).reshape(Q_RANK, heads * QK_PAD).astype(BF16)
    return win, wa, o_w_ukv.astype(BF16)


def kernel(x, c, ctx, c_ctx, w_ada, b_ada, norm_g, e_w_in, e_conv_w, e_conv_b, e_ln_a_g, e_ln_a_b, e_ln_v_g, e_ln_v_b, e_w_s, e_b_s, e_w_out, e_ffn_w1, e_ffn_w3, e_ffn_w2, o_w_in, o_g_q, o_g_kv, o_w_uq, o_w_ukv, o_w_o, o_router, o_exp_w1, o_exp_w3, o_exp_w2, final_g):
    b, t, d = x.shape
    tc = ctx.shape[1]
    n, nc = b * t, b * tc
    depth = w_ada.shape[0]
    assert depth == 2 and t % GRID_W == 0 and t % CHUNK == 0 and tc % CHUNK == 0
    assert d == SUBLANE * LANE

    tm_x = min(512, t)
    tm_c = min(256, tc)

    rows = 16
    cc = jnp.concatenate([c, c_ctx[None], jnp.zeros((rows - b - 1, d), F32)], axis=0)
    mods = _ada(cc, w_ada, b_ada).reshape(depth, rows, 6, d)
    xmap = lambda tm: (lambda i: (i // (t // tm), 0, 0))
    cmap = lambda i: (b, 0, 0)

    x2 = x.reshape(n, d)
    c2 = ctx.reshape(nc, d)
    row1 = lambda v: v.reshape(1, -1)

    w_in = e_w_in[0].astype(BF16)
    w_out = e_w_out[0].astype(BF16)
    w_s = e_w_s[0].astype(BF16)
    ch = e_conv_w.shape[-1]
    bias_map = jnp.repeat(e_b_s[0].T, ch // B_HEADS, axis=1)
    conv_w = jnp.concatenate([e_conv_w[0], jnp.zeros((1, ch), F32)], axis=0)
    f1 = e_ffn_w1[0].astype(BF16)
    f3 = e_ffn_w3[0].astype(BF16)
    f2 = e_ffn_w2[0].astype(BF16)
    dff = f1.shape[-1]
    tf = 2 * LANE
    assert dff % tf == 0

    def layer0(s2, seq, tm, mod_a, mod_f, mod_map, tm_f):
        a, u, v = _inproj(s2, mod_a, mod_map(tm), row1(norm_g[0, 0]), w_in,
                          row1(e_ln_v_g[0]), row1(e_ln_v_b[0]), tm)
        s2 = _mixer(a, u, v, s2, mod_a, mod_map(tm), conv_w, row1(e_conv_b[0]),
                    row1(e_ln_a_g[0]), row1(e_ln_a_b[0]), w_s, bias_map, w_out, tm, seq)
        return _ffn(s2, mod_f, mod_map(tm_f), row1(norm_g[0, 1]), f1, f3, f2, tm_f, tf)

    m0a, m0f = mods[0, :, 0:3], mods[0, :, 3:6]
    x2 = layer0(x2, t, tm_x, m0a, m0f, xmap, tm_x)
    c2 = layer0(c2, tc, tm_c, m0a, m0f, lambda tm: cmap, tm_c)

    m1a, m1f = mods[1, :, 0:3], mods[1, :, 3:6]
    win, wa, wukv = _mla_weights(o_w_in[0], o_w_uq[0], o_w_ukv[0])
    cos, sin = _rope_tables(t)
    ones = jnp.concatenate([jnp.ones((tc, QK_ROPE), F32), jnp.zeros((tc, LANE - QK_ROPE), F32)], axis=1)
    zeros = jnp.zeros((tc, LANE), F32)
    g10 = row1(norm_g[1, 0])
    proj = functools.partial(_mla_proj, g=g10, win=win, gq=row1(o_g_q[0]),
                             gkv=row1(o_g_kv[0]), wa=wa, wukv=wukv)
    q_x, k_x, v_x = proj(x2.reshape(b, t, d), m1a, xmap(tm_x), cos=cos, sin=sin, tm=tm_x)
    _, k_c, v_c = proj(c2.reshape(b, tc, d), m1a, cmap, cos=ones, sin=zeros, tm=tm_c)
    att = _attention(q_x, k_c, v_c, k_x, v_x, tq=min(4096, t), tk=min(256, t))

    r_pad = jnp.concatenate([o_router[0], jnp.zeros((d, LANE - N_EXPERTS), F32)], axis=1)
    r_hi = r_pad.astype(BF16)
    r_lo = (r_pad - r_hi.astype(F32)).astype(BF16)
    x2, hp, sel = _outproj_router(att.reshape(n, -1), o_w_o[0].astype(BF16), x2, m1a, m1f, xmap(tm_x),
                                  row1(norm_g[1, 1]), r_hi, r_lo, tm_x)
    tm_e = 512
    pos, pad_slots, tile_expert, n_used, n_tiles = _routing_tables(sel, tm_e)
    tm_s = min(2048, t)
    hs = _scatter_rows(pos.reshape(n // tm_s, 1, 2 * tm_s), pad_slots, hp, n_tiles * tm_e, tm_s)
    dexp = o_exp_w1.shape[-1]
    tf_e = 2 * LANE
    assert dexp % tf_e == 0
    ys = _moe_ffn(tile_expert, n_used, hs, o_exp_w1[0].astype(BF16), o_exp_w3[0].astype(BF16),
                  o_exp_w2[0].astype(BF16), tm_e, tf_e)
    tm_k = min(512, t)
    out = _combine(pos.reshape(n // tm_k, 1, 2 * tm_k), x2, m1f, xmap(tm_k), sel, row1(final_g), ys, tm_k)
    return out.reshape(b, t, d)
```

```python
import functools

import jax
import jax.numpy as jnp
from jax import lax
from jax.experimental import pallas as pl
from jax.experimental.pallas import tpu as pltpu

F32 = jnp.float32
BF16 = jnp.bfloat16

EPS = 1e-6
GRID_W = 64
CONV_K = 31
CHUNK = 128
B_HEADS = 8
QK_NOPE = 128
QK_ROPE = 64
V_HD = 128
Q_RANK = 384
KV_RANK = 256
ROPE_BASE = 10000.0
N_EXPERTS = 8
ATTN_SCALE = (QK_NOPE + QK_ROPE) ** -0.5
LOG2_E = 1.4426950408889634
Q_SCALE = ATTN_SCALE * LOG2_E

LANE = 128
SUBLANE = 8
HALO = 16
CONV_ROWS = 64
QK_PAD = 256
ROW_SPLIT = 2
VMEM_LIMIT = 52 * 1024 * 1024


def _cparams(sem):
    return pltpu.CompilerParams(dimension_semantics=sem, vmem_limit_bytes=VMEM_LIMIT)


def _silu(x):
    return x * jax.nn.sigmoid(x)


def _rms_mod(x, g, shift, scale):
    y = x * lax.rsqrt(jnp.mean(x * x, axis=-1, keepdims=True) + EPS) * g
    return y * (1.0 + scale) + shift


def _rms(x, g):
    return x * lax.rsqrt(jnp.mean(x * x, axis=-1, keepdims=True) + EPS) * g


def _dot(a, b):
    return jnp.dot(a, b, preferred_element_type=F32)


def _ada_kernel(cc_ref, w_ref, b_ref, o_ref):
    s = _silu(cc_ref[...]).astype(BF16)
    o_ref[0] = _dot(s, w_ref[0].astype(BF16)) + b_ref[0]


def _ada(cc, w_ada, b_ada):
    depth, d, d6 = w_ada.shape
    rows = cc.shape[0]
    tn = 1536
    return pl.pallas_call(
        _ada_kernel,
        grid=(depth, d6 // tn),
        in_specs=[
            pl.BlockSpec((rows, d), lambda i, j: (0, 0)),
            pl.BlockSpec((1, d, tn), lambda i, j: (i, 0, j)),
            pl.BlockSpec((1, 1, tn), lambda i, j: (i, 0, j)),
        ],
        out_specs=pl.BlockSpec((1, rows, tn), lambda i, j: (i, 0, j)),
        out_shape=jax.ShapeDtypeStruct((depth, rows, d6), F32),
        compiler_params=_cparams(("parallel", "parallel")),
        name="ada",
    )(cc, w_ada, b_ada.reshape(depth, 1, d6))


def _inproj_kernel(x_ref, mod_ref, g_ref, w_ref, lng_ref, lnb_ref, a_ref, u_ref, v_ref):
    ch = a_ref.shape[1]
    h = _rms_mod(x_ref[...], g_ref[...], mod_ref[0, 0:1, :], mod_ref[0, 1:2, :]).astype(BF16)
    z = _dot(h, w_ref[...])
    a = z[:, 0:ch] * jax.nn.sigmoid(z[:, ch:2 * ch])
    u = jax.nn.gelu(z[:, 2 * ch:3 * ch])
    v = jax.nn.gelu(z[:, 3 * ch:4 * ch])
    mu = jnp.mean(v, axis=-1, keepdims=True)
    vc = v - mu
    var = jnp.mean(vc * vc, axis=-1, keepdims=True)
    v = vc * lax.rsqrt(var + EPS) * lng_ref[...] + lnb_ref[...]
    a_ref[...] = a.astype(BF16)
    u_ref[...] = u.astype(BF16)
    v_ref[...] = v.astype(BF16)


def _inproj(x2, mod, mod_map, g, w_in, ln_g, ln_b, tm):
    n, d = x2.shape
    ch = w_in.shape[1] // 4
    row = lambda t: (t, 0)
    const = lambda t: (0, 0)
    out = jax.ShapeDtypeStruct((n, ch), BF16)
    return pl.pallas_call(
        _inproj_kernel,
        grid=(n // tm,),
        in_specs=[
            pl.BlockSpec((tm, d), row),
            pl.BlockSpec((1, 3, d), mod_map),
            pl.BlockSpec((1, d), const),
            pl.BlockSpec(w_in.shape, const),
            pl.BlockSpec((1, ch), const),
            pl.BlockSpec((1, ch), const),
        ],
        out_specs=[pl.BlockSpec((tm, ch), row)] * 3,
        out_shape=[out, out, out],
        compiler_params=_cparams(("parallel",)),
        name="l0_inproj",
    )(x2, mod, g, w_in, ln_g, ln_b)


def _mixer_kernel(a_ref, ap_ref, an_ref, u_ref, v_ref, x_ref, mod_ref, cw_ref, cb_ref,
                  lag_ref, lab_ref, ws_ref, bs_ref, wo_ref, o_ref,
                  ext_ref, sh_ref, conv_ref, cat_ref, *, tiles_per_seq):
    tm, ch = a_ref.shape
    t = pl.program_id(0)
    first = (t % tiles_per_seq) == 0
    last = (t % tiles_per_seq) == tiles_per_seq - 1

    ext_ref[0:HALO, :] = jnp.where(first, 0.0, ap_ref[...].astype(F32))
    ext_ref[HALO:HALO + tm, :] = a_ref[...].astype(F32)
    ext_ref[HALO + tm:, :] = jnp.where(last, 0.0, an_ref[...].astype(F32))
    span = tm + 24
    for r in range(1, 8):
        sh_ref[r - 1] = ext_ref[r:r + span, :]

    off0 = HALO - CONV_K // 2
    cb = cb_ref[...]

    def conv_block(rb, carry):
        base = pl.multiple_of(rb * CONV_ROWS, CONV_ROWS)
        acc = jnp.broadcast_to(cb, (CONV_ROWS, ch))
        for k in range(CONV_K):
            q, r = divmod(off0 + k, 8)
            start = pl.multiple_of(base + 8 * q, 8)
            if r == 0:
                win = ext_ref[pl.ds(start, CONV_ROWS), :]
            else:
                win = sh_ref[r - 1, pl.ds(start, CONV_ROWS), :]
            acc = acc + cw_ref[k:k + 1, :] * win
        conv_ref[pl.ds(base, CONV_ROWS), :] = acc
        return carry

    lax.fori_loop(0, tm // CONV_ROWS, conv_block, 0)

    acc = conv_ref[...]
    mu = jnp.mean(acc, axis=-1, keepdims=True)
    c = acc - mu
    var = jnp.mean(c * c, axis=-1, keepdims=True)
    y = c * lax.rsqrt(var + EPS) * lag_ref[...] + lab_ref[...]
    cat_ref[:, 0:ch] = _silu(y).astype(BF16)

    lane = lax.broadcasted_iota(jnp.int32, (CHUNK, LANE), 1)
    lo_half = lane < (LANE // 2)
    for c in range(tm // CHUNK):
        rows = slice(c * CHUNK, (c + 1) * CHUNK)
        for p in range(ch // LANE):
            cols = slice(p * LANE, (p + 1) * LANE)
            vv = v_ref[rows, cols]
            s0 = _dot(ws_ref[2 * p], vv)
            s1 = _dot(ws_ref[2 * p + 1], vv)
            sv = jnp.where(lo_half, s0, s1) + bs_ref[:, cols]
            bo = u_ref[rows, cols].astype(F32) * sv
            cat_ref[rows, ch + p * LANE:ch + (p + 1) * LANE] = bo.astype(BF16)

    y = _dot(cat_ref[...], wo_ref[...])
    o_ref[...] = x_ref[...] + mod_ref[0, 2:3, :] * y


def _mixer(a, u, v, x2, mod, mod_map, conv_w, conv_b, la_g, la_b, w_s, bias_map, w_out, tm, seq):
    n, d = x2.shape
    ch = a.shape[1]
    hb = tm // HALO
    nhb = n // HALO
    row = lambda t: (t, 0)
    const = lambda t: (0, 0)
    kern = functools.partial(_mixer_kernel, tiles_per_seq=seq // tm)
    return pl.pallas_call(
        kern,
        grid=(n // tm,),
        in_specs=[
            pl.BlockSpec((tm, ch), row),
            pl.BlockSpec((HALO, ch), lambda t: (jnp.maximum(t * hb - 1, 0), 0)),
            pl.BlockSpec((HALO, ch), lambda t: (jnp.minimum((t + 1) * hb, nhb - 1), 0)),
            pl.BlockSpec((tm, ch), row),
            pl.BlockSpec((tm, ch), row),
            pl.BlockSpec((tm, d), row),
            pl.BlockSpec((1, 3, d), mod_map),
            pl.BlockSpec(conv_w.shape, const),
            pl.BlockSpec((1, ch), const),
            pl.BlockSpec((1, ch), const),
            pl.BlockSpec((1, ch), const),
            pl.BlockSpec(w_s.shape, lambda t: (0, 0, 0)),
            pl.BlockSpec(bias_map.shape, const),
            pl.BlockSpec(w_out.shape, const),
        ],
        out_specs=pl.BlockSpec((tm, d), row),
        out_shape=jax.ShapeDtypeStruct((n, d), F32),
        scratch_shapes=[
            pltpu.VMEM((tm + 2 * HALO, ch), F32),
            pltpu.VMEM((7, tm + 24, ch), F32),
            pltpu.VMEM((tm, ch), F32),
            pltpu.VMEM((tm, 2 * ch), BF16),
        ],
        compiler_params=_cparams(("parallel",)),
        name="l0_mixer",
    )(a, a, a, u, v, x2, mod, conv_w, conv_b, la_g, la_b, w_s, bias_map, w_out)


def _ffn_kernel(x_ref, mod_ref, g_ref, w1_ref, w3_ref, w2_ref, o_ref, *, tf):
    x = x_ref[...]
    h = _rms_mod(x, g_ref[...], mod_ref[0, 0:1, :], mod_ref[0, 1:2, :]).astype(BF16)
    acc = None
    for c in range(w1_ref.shape[1] // tf):
        cols = slice(c * tf, (c + 1) * tf)
        act = _silu(_dot(h, w1_ref[:, cols])) * _dot(h, w3_ref[:, cols])
        part = _dot(act.astype(BF16), w2_ref[cols, :])
        acc = part if acc is None else acc + part
    o_ref[...] = x + mod_ref[0, 2:3, :] * acc


def _ffn(x2, mod, mod_map, g, w1, w3, w2, tm, tf):
    n, d = x2.shape
    row = lambda t: (t, 0)
    resident = lambda w: pl.BlockSpec(w.shape, lambda t: (0, 0), pipeline_mode=pl.Buffered(1))
    return pl.pallas_call(
        functools.partial(_ffn_kernel, tf=tf),
        grid=(n // tm,),
        in_specs=[
            pl.BlockSpec((tm, d), row),
            pl.BlockSpec((1, 3, d), mod_map),
            pl.BlockSpec((1, d), lambda t: (0, 0)),
            resident(w1), resident(w3), resident(w2),
        ],
        out_specs=pl.BlockSpec((tm, d), row),
        out_shape=jax.ShapeDtypeStruct((n, d), F32),
        compiler_params=_cparams(("parallel",)),
        name="ffn",
    )(x2, mod, g, w1, w3, w2)


def _rotate(t, cos, sin):
    return t * cos + pltpu.roll(t, LANE // 2, 1) * sin


def _mla_proj_kernel(x_ref, mod_ref, g_ref, win_ref, gq_ref, gkv_ref, wa_ref, wukv_ref, cos_ref, sin_ref,
                     q_ref, k_ref, v_ref):
    heads = q_ref.shape[1]
    h = _rms_mod(x_ref[...], g_ref[...], mod_ref[0, 0:1, :], mod_ref[0, 1:2, :]).astype(BF16)
    cos = cos_ref[...]
    sin = sin_ref[...]
    z = _dot(h, win_ref[...])
    kr = _rotate(z[:, Q_RANK + KV_RANK:], cos, sin).astype(BF16)
    qn = _rms(z[:, :Q_RANK], gq_ref[...]).astype(BF16)
    qa = _dot(qn, wa_ref[...])
    kvn = _rms(z[:, Q_RANK:Q_RANK + KV_RANK], gkv_ref[...]).astype(BF16)
    kv = _dot(kvn, wukv_ref[...])
    for hd in range(heads):
        q_ref[0, hd, :, 0:LANE] = (qa[:, hd * QK_PAD:hd * QK_PAD + LANE] * Q_SCALE).astype(BF16)
        qr = _rotate(qa[:, hd * QK_PAD + LANE:(hd + 1) * QK_PAD], cos, sin)
        q_ref[0, hd, :, LANE:QK_PAD] = (qr * Q_SCALE).astype(BF16)
        k_ref[0, hd, :, 0:LANE] = kv[:, hd * 2 * LANE:hd * 2 * LANE + LANE].astype(BF16)
        k_ref[0, hd, :, LANE:QK_PAD] = kr
        v_ref[0, hd, :, :] = kv[:, hd * 2 * LANE + LANE:(hd + 1) * 2 * LANE].astype(BF16)


def _mla_proj(x3, mod, mod_map, g, win, gq, gkv, wa, wukv, cos, sin, tm):
    b, t, d = x3.shape
    heads = wukv.shape[1] // (2 * LANE)
    tps = t // tm
    row = lambda i: (i // tps, i % tps, 0)
    const = lambda i: (0, 0)
    tab = lambda i: (i % tps, 0)
    hrow = lambda i: (i // tps, 0, i % tps, 0)
    return pl.pallas_call(
        _mla_proj_kernel,
        grid=(b * tps,),
        in_specs=[
            pl.BlockSpec((None, tm, d), row),
            pl.BlockSpec((1, 3, d), mod_map),
            pl.BlockSpec((1, d), const),
            pl.BlockSpec(win.shape, const),
            pl.BlockSpec(gq.shape, const),
            pl.BlockSpec(gkv.shape, const),
            pl.BlockSpec(wa.shape, const),
            pl.BlockSpec(wukv.shape, const),
            pl.BlockSpec((tm, LANE), tab),
            pl.BlockSpec((tm, LANE), tab),
        ],
        out_specs=[
            pl.BlockSpec((1, heads, tm, QK_PAD), hrow),
            pl.BlockSpec((1, heads, tm, QK_PAD), hrow),
            pl.BlockSpec((1, heads, tm, LANE), hrow),
        ],
        out_shape=[
            jax.ShapeDtypeStruct((b, heads, t, QK_PAD), BF16),
            jax.ShapeDtypeStruct((b, heads, t, QK_PAD), BF16),
            jax.ShapeDtypeStruct((b, heads, t, LANE), BF16),
        ],
        compiler_params=_cparams(("parallel",)),
        name="l1_proj",
    )(x3, mod, g, win, gq, gkv, wa, wukv, cos, sin)


def _attn_kernel(q_ref, kc_ref, vc_ref, kx_ref, vx_ref, o_ref, *, tk):
    q = q_ref[0, 0]
    tq = q.shape[0]

    def step(k, v, m, l, acc):
        s = lax.dot_general(q, k, (((1,), (1,)), ((), ())), preferred_element_type=F32)
        m_new = jnp.maximum(m, jnp.max(s, axis=-1, keepdims=True))
        alpha = jnp.exp2(m - m_new)
        p = jnp.exp2(s - m_new)
        l = alpha * l + jnp.sum(p, axis=-1, keepdims=True)
        acc = alpha * acc + _dot(p.astype(BF16), v)
        return m_new, l, acc

    m = jnp.full((tq, 1), -jnp.inf, F32)
    l = jnp.zeros((tq, 1), F32)
    acc = jnp.zeros((tq, vc_ref.shape[-1]), F32)
    m, l, acc = step(kc_ref[0, 0], vc_ref[0, 0], m, l, acc)
    for c in range(kx_ref.shape[2] // tk):
        m, l, acc = step(kx_ref[0, 0, c * tk:(c + 1) * tk, :], vx_ref[0, 0, c * tk:(c + 1) * tk, :],
                         m, l, acc)
    o_ref[0] = (acc / l).astype(BF16)


def _attention(q, kc, vc, kx, vx, tq, tk):
    b, heads, t, _ = q.shape
    tc = kc.shape[2]
    dv = vx.shape[-1]
    return pl.pallas_call(
        functools.partial(_attn_kernel, tk=tk),
        grid=(b, heads, t // tq),
        in_specs=[
            pl.BlockSpec((1, 1, tq, QK_PAD), lambda i, h, j: (i, h, j, 0)),
            pl.BlockSpec((1, 1, tc, QK_PAD), lambda i, h, j: (i, h, 0, 0)),
            pl.BlockSpec((1, 1, tc, dv), lambda i, h, j: (i, h, 0, 0)),
            pl.BlockSpec((1, 1, t, QK_PAD), lambda i, h, j: (i, h, 0, 0)),
            pl.BlockSpec((1, 1, t, dv), lambda i, h, j: (i, h, 0, 0)),
        ],
        out_specs=pl.BlockSpec((1, tq, dv), lambda i, h, j: (i, j, h)),
        out_shape=jax.ShapeDtypeStruct((b, t, heads * dv), BF16),
        compiler_params=_cparams(("parallel", "parallel", "arbitrary")),
        name="l1_attn",
    )(q, kc, vc, kx, vx)


def _outproj_router_kernel(a_ref, w_ref, x_ref, moda_ref, modf_ref, g_ref, rh_ref, rl_ref,
                           xo_ref, h_ref, sel_ref):
    tm = x_ref.shape[0]
    sub = tm // ROW_SPLIT
    for blk in range(ROW_SPLIT):
        rows = slice(blk * sub, (blk + 1) * sub)
        x = x_ref[rows, :] + moda_ref[0, 2:3, :] * _dot(a_ref[rows, :], w_ref[...])
        xo_ref[rows, :] = x
        h = _rms_mod(x, g_ref[...], modf_ref[0, 0:1, :], modf_ref[0, 1:2, :])
        _to_row_tiles(h_ref, blk * sub, h)
        hh = h.astype(BF16)
        hl = (h - hh.astype(F32)).astype(BF16)
        rh = rh_ref[...]
        logits = _dot(hh, rh) + (_dot(hl, rh) + _dot(hh, rl_ref[...]))
        lane = lax.broadcasted_iota(jnp.int32, logits.shape, 1).astype(F32)
        neg = -jnp.inf
        logits = jnp.where(lane < N_EXPERTS, logits, neg)
        v1 = jnp.max(logits, axis=-1, keepdims=True)
        i1 = jnp.min(jnp.where(logits == v1, lane, float(LANE)), axis=-1, keepdims=True)
        rest = jnp.where(lane == i1, neg, logits)
        v2 = jnp.max(rest, axis=-1, keepdims=True)
        i2 = jnp.min(jnp.where(rest == v2, lane, float(LANE)), axis=-1, keepdims=True)
        e2 = jnp.exp(v2 - v1)
        w1 = 1.0 / (1.0 + e2)
        w2 = e2 / (1.0 + e2)
        sel_ref[rows, :] = (jnp.where(lane == 0.0, i1, 0.0) + jnp.where(lane == 1.0, i2, 0.0)
                            + jnp.where(lane == 2.0, w1, 0.0) + jnp.where(lane == 3.0, w2, 0.0))


def _outproj_router(a, w, x2, mod_a, mod_f, mod_map, g, r_hi, r_lo, tm):
    n, d = x2.shape
    row = lambda t: (t, 0)
    const = lambda t: (0, 0)
    return pl.pallas_call(
        _outproj_router_kernel,
        grid=(n // tm,),
        in_specs=[
            pl.BlockSpec((tm, a.shape[1]), row),
            pl.BlockSpec(w.shape, const),
            pl.BlockSpec((tm, d), row),
            pl.BlockSpec((1, 3, d), mod_map),
            pl.BlockSpec((1, 3, d), mod_map),
            pl.BlockSpec((1, d), const),
            pl.BlockSpec(r_hi.shape, const),
            pl.BlockSpec(r_lo.shape, const),
        ],
        out_specs=[pl.BlockSpec((tm, d), row), pl.BlockSpec((tm * SUBLANE, LANE), row),
                   pl.BlockSpec((tm, LANE), row)],
        out_shape=[jax.ShapeDtypeStruct((n, d), F32), jax.ShapeDtypeStruct((n * SUBLANE, LANE), F32),
                   jax.ShapeDtypeStruct((n, LANE), F32)],
        compiler_params=_cparams(("parallel",)),
        name="l1_outproj_router",
    )(a, w, x2, mod_a, mod_f, g, r_hi, r_lo)


def _routing_tables(sel, tm):
    n = sel.shape[0]
    ids = sel[:, 0:2].astype(jnp.int32).reshape(2 * n)
    onehot = (ids[:, None] == jnp.arange(N_EXPERTS, dtype=jnp.int32)[None, :]).astype(jnp.int32)
    csum = jnp.cumsum(onehot, axis=0)
    rank = jnp.sum(csum * onehot, axis=1) - 1
    tiles = (csum[-1] + tm - 1) // tm
    tile_end = jnp.cumsum(tiles)
    start = (tile_end - tiles) * tm
    pos = start[ids] + rank
    n_tiles = (2 * n) // tm + N_EXPERTS
    tile_expert = jnp.sum(jnp.arange(n_tiles, dtype=jnp.int32)[:, None] >= tile_end[None, :], axis=1)
    tile_expert = jnp.minimum(tile_expert, N_EXPERTS - 1).astype(jnp.int32)
    n_pad = n_tiles * tm - 2 * n
    region_start = jnp.concatenate([start + csum[-1], tile_end[-1:] * tm])
    region_size = jnp.concatenate([tiles * tm - csum[-1], n_tiles * tm - tile_end[-1:] * tm])
    region_end = jnp.cumsum(region_size)
    q = jnp.arange(n_pad, dtype=jnp.int32)
    region = jnp.sum(q[:, None] >= region_end[None, :], axis=1)
    pad_slots = region_start[region] + q - (region_end - region_size)[region]
    return (pos.astype(jnp.int32), pad_slots.astype(jnp.int32).reshape(1, n_pad), tile_expert,
            tile_end[-1:].astype(jnp.int32), n_tiles)


def _to_row_tiles(ref, first_row, x):
    m = x.shape[0]
    for s in range(SUBLANE):
        ref[pl.ds(first_row * SUBLANE + s, m, stride=SUBLANE), :] = x[:, s * LANE:(s + 1) * LANE]


def _from_row_tiles(ref, m):
    return jnp.concatenate([ref[pl.ds(s, m, stride=SUBLANE), :] for s in range(SUBLANE)], axis=1)


def _row_copy(src, src_row, dst, dst_row, sem):
    def start(row):
        return row * SUBLANE if isinstance(row, int) else pl.multiple_of(row * SUBLANE, SUBLANE)

    return pltpu.make_async_copy(src.at[pl.ds(start(src_row), SUBLANE)],
                                 dst.at[pl.ds(start(dst_row), SUBLANE)], sem)


def _scatter_kernel(pos_ref, pad_ref, h_ref, hs_ref, zero_ref, sem):
    tm = h_ref.shape[0] // SUBLANE
    n_pad = pad_ref.shape[1]

    def drain(n_rows):
        for _ in range(n_rows // tm):
            pltpu.make_async_copy(h_ref, hs_ref.at[pl.ds(0, tm * SUBLANE)], sem).wait()

    @pl.when(pl.program_id(0) == 0)
    def _():
        zero_ref[...] = jnp.zeros_like(zero_ref)

        def issue_pad(q, carry):
            _row_copy(zero_ref, 0, hs_ref, pad_ref[0, q], sem).start()
            return carry

        lax.fori_loop(0, n_pad, issue_pad, 0, unroll=8)
        drain(n_pad)

    def issue(r, carry):
        _row_copy(h_ref, r, hs_ref, pos_ref[0, 2 * r], sem).start()
        _row_copy(h_ref, r, hs_ref, pos_ref[0, 2 * r + 1], sem).start(priority=1)
        return carry

    lax.fori_loop(0, tm, issue, 0, unroll=8)
    drain(2 * tm)


def _scatter_rows(pos2, pad_slots, hp, n_slots, tm):
    n = hp.shape[0] // SUBLANE
    assert pad_slots.shape[1] % tm == 0
    return pl.pallas_call(
        _scatter_kernel,
        grid=(n // tm,),
        in_specs=[
            pl.BlockSpec((None, 1, 2 * tm), lambda t: (t, 0, 0), memory_space=pltpu.SMEM),
            pl.BlockSpec(pad_slots.shape, lambda t: (0, 0), memory_space=pltpu.SMEM),
            pl.BlockSpec((tm * SUBLANE, LANE), lambda t: (t, 0)),
        ],
        out_specs=pl.BlockSpec(memory_space=pl.ANY),
        out_shape=jax.ShapeDtypeStruct((n_slots * SUBLANE, LANE), hp.dtype),
        scratch_shapes=[pltpu.VMEM((SUBLANE, LANE), hp.dtype), pltpu.SemaphoreType.DMA(())],
        compiler_params=_cparams(("arbitrary",)),
        name="moe_scatter",
    )(pos2, pad_slots, hp)


def _moe_ffn_kernel(te_ref, nt_ref, hs_ref, w1_ref, w3_ref, w2_ref, y_ref, *, tf):
    del te_ref
    tm = hs_ref.shape[0] // SUBLANE
    valid = pl.program_id(0) < nt_ref[0]

    @pl.when(valid)
    def _():
        h = _from_row_tiles(hs_ref, tm).astype(BF16)
        acc = None
        for c in range(w1_ref.shape[2] // tf):
            cols = slice(c * tf, (c + 1) * tf)
            act = _silu(_dot(h, w1_ref[0, :, cols])) * _dot(h, w3_ref[0, :, cols])
            part = _dot(act.astype(BF16), w2_ref[0, cols, :])
            acc = part if acc is None else acc + part
        _to_row_tiles(y_ref, 0, acc)

    @pl.when(jnp.logical_not(valid))
    def _():
        y_ref[...] = jnp.zeros_like(y_ref)


def _moe_ffn(tile_expert, n_tiles_used, hs, w1, w3, w2, tm, tf):
    p = hs.shape[0] // SUBLANE
    _, d, dff = w1.shape
    expert = lambda j, te, nt: (te[j], 0, 0)
    resident = lambda shape: pl.BlockSpec(shape, expert, pipeline_mode=pl.Buffered(1))
    grid_spec = pltpu.PrefetchScalarGridSpec(
        num_scalar_prefetch=2,
        grid=(p // tm,),
        in_specs=[
            pl.BlockSpec((tm * SUBLANE, LANE), lambda j, te, nt: (j, 0)),
            resident((1, d, dff)), resident((1, d, dff)), resident((1, dff, d)),
        ],
        out_specs=pl.BlockSpec((tm * SUBLANE, LANE), lambda j, te, nt: (j, 0)),
    )
    return pl.pallas_call(
        functools.partial(_moe_ffn_kernel, tf=tf),
        grid_spec=grid_spec,
        out_shape=jax.ShapeDtypeStruct(hs.shape, F32),
        compiler_params=_cparams(("arbitrary",)),
        name="moe_ffn",
    )(tile_expert, n_tiles_used, hs, w1, w3, w2)


def _combine_kernel(pos_ref, nxt_ref, x_ref, mod_ref, sel_ref, fg_ref, ys_ref, o_ref, buf, sem):
    tm = x_ref.shape[0]
    t = pl.program_id(0)
    slot = t % 2

    def gather(p_ref, s):
        def issue(r, carry):
            _row_copy(ys_ref, p_ref[0, 2 * r], buf.at[s, 0], r, sem.at[s]).start()
            _row_copy(ys_ref, p_ref[0, 2 * r + 1], buf.at[s, 1], r, sem.at[s]).start(priority=1)
            return carry

        lax.fori_loop(0, tm, issue, 0, unroll=8)

    @pl.when(t == 0)
    def _():
        gather(pos_ref, 0)

    @pl.when(t + 1 < pl.num_programs(0))
    def _():
        gather(nxt_ref, 1 - slot)

    for k in range(2):
        pltpu.make_async_copy(ys_ref.at[pl.ds(0, tm * SUBLANE)], buf.at[slot, k], sem.at[slot]).wait()
    sel = sel_ref[...]
    y = (sel[:, 2:3] * _from_row_tiles(buf.at[slot, 0], tm)
         + sel[:, 3:4] * _from_row_tiles(buf.at[slot, 1], tm))
    o_ref[...] = _rms(x_ref[...] + mod_ref[0, 2:3, :] * y, fg_ref[...])


def _combine(pos2, x2, mod, mod_map, sel, final_g, ys, tm):
    n, d = x2.shape
    steps = n // tm
    row = lambda t: (t, 0)
    return pl.pallas_call(
        _combine_kernel,
        grid=(steps,),
        in_specs=[
            pl.BlockSpec((None, 1, 2 * tm), lambda t: (t, 0, 0), memory_space=pltpu.SMEM),
            pl.BlockSpec((None, 1, 2 * tm), lambda t: (jnp.minimum(t + 1, steps - 1), 0, 0),
                         memory_space=pltpu.SMEM),
            pl.BlockSpec((tm, d), row),
            pl.BlockSpec((1, 3, d), mod_map),
            pl.BlockSpec((tm, LANE), row),
            pl.BlockSpec((1, d), lambda t: (0, 0)),
            pl.BlockSpec(memory_space=pl.ANY),
        ],
        out_specs=pl.BlockSpec((tm, d), row),
        out_shape=jax.ShapeDtypeStruct((n, d), F32),
        scratch_shapes=[pltpu.VMEM((2, 2, tm * SUBLANE, LANE), F32), pltpu.SemaphoreType.DMA((2,))],
        compiler_params=_cparams(("arbitrary",)),
        name="moe_combine",
    )(pos2, pos2, x2, mod, sel, final_g, ys)


def _rope_tables(t):
    freqs = QK_ROPE // 4
    rows = t // GRID_W
    row = jnp.repeat(jnp.arange(rows), GRID_W).astype(F32)
    col = jnp.tile(jnp.arange(GRID_W), rows).astype(F32)
    inv_freq = ROPE_BASE ** (-jnp.arange(freqs, dtype=F32) / freqs)
    ang = jnp.stack([row[:, None] * inv_freq, col[:, None] * inv_freq], axis=1)
    cos = jnp.cos(ang)
    sin = jnp.sin(ang)
    c64 = jnp.stack([cos, cos], axis=2).reshape(t, QK_ROPE)
    s64 = jnp.stack([-sin, sin], axis=2).reshape(t, QK_ROPE)
    pad = jnp.zeros((t, LANE - QK_ROPE), F32)
    return jnp.concatenate([c64, pad], axis=1), jnp.concatenate([s64, pad], axis=1)


def _rope_partner_index():
    freqs = QK_ROPE // 4
    d = jnp.arange(QK_ROPE)
    half = (d // freqs) % 2
    return jnp.where(half == 0, d + freqs, d - freqs)


def _mla_weights(o_w_in, o_w_uq, o_w_ukv):
    heads = o_w_ukv.shape[1] // (QK_NOPE + V_HD)
    pidx = _rope_partner_index()
    pe = o_w_in[:, Q_RANK + KV_RANK:]
    win = jnp.concatenate([o_w_in, pe[:, pidx]], axis=1).astype(BF16)
    uq = o_w_uq.reshape(Q_RANK, heads, QK_NOPE + QK_ROPE)
    nope, rope = uq[..., :QK_NOPE], uq[..., QK_NOPE:]
    wa = jnp.concatenate([nope, rope, rope[..., pidx]], axis=-1).reshape(Q_RANK, heads * QK_PAD).astype(BF16)
    return win, wa, o_w_ukv.astype(BF16)


def kernel(x, c, ctx, c_ctx, w_ada, b_ada, norm_g, e_w_in, e_conv_w, e_conv_b, e_ln_a_g, e_ln_a_b, e_ln_v_g, e_ln_v_b, e_w_s, e_b_s, e_w_out, e_ffn_w1, e_ffn_w3, e_ffn_w2, o_w_in, o_g_q, o_g_kv, o_w_uq, o_w_ukv, o_w_o, o_router, o_exp_w1, o_exp_w3, o_exp_w2, final_g):
    b, t, d = x.shape
    tc = ctx.shape[1]
    n, nc = b * t, b * tc
    depth = w_ada.shape[0]
    assert depth == 2 and t % GRID_W == 0 and t % CHUNK == 0 and tc % CHUNK == 0
    assert d == SUBLANE * LANE

    tm_x = min(1024, t)
    tm_c = min(256, tc)

    rows = 16
    cc = jnp.concatenate([c, c_ctx[None], jnp.zeros((rows - b - 1, d), F32)], axis=0)
    mods = _ada(cc, w_ada, b_ada).reshape(depth, rows, 6, d)
    xmap = lambda tm: (lambda i: (i // (t // tm), 0, 0))
    cmap = lambda i: (b, 0, 0)

    x2 = x.reshape(n, d)
    c2 = ctx.reshape(nc, d)
    row1 = lambda v: v.reshape(1, -1)

    w_in = e_w_in[0].astype(BF16)
    w_out = e_w_out[0].astype(BF16)
    w_s = e_w_s[0].astype(BF16)
    ch = e_conv_w.shape[-1]
    bias_map = jnp.repeat(e_b_s[0].T, ch // B_HEADS, axis=1)
    conv_w = jnp.concatenate([e_conv_w[0], jnp.zeros((1, ch), F32)], axis=0)
    f1 = e_ffn_w1[0].astype(BF16)
    f3 = e_ffn_w3[0].astype(BF16)
    f2 = e_ffn_w2[0].astype(BF16)
    dff = f1.shape[-1]
    tf = 2 * LANE
    assert dff % tf == 0

    def layer0(s2, seq, tm, mod_a, mod_f, mod_map, tm_f):
        a, u, v = _inproj(s2, mod_a, mod_map(tm), row1(norm_g[0, 0]), w_in,
                          row1(e_ln_v_g[0]), row1(e_ln_v_b[0]), tm)
        s2 = _mixer(a, u, v, s2, mod_a, mod_map(tm), conv_w, row1(e_conv_b[0]),
                    row1(e_ln_a_g[0]), row1(e_ln_a_b[0]), w_s, bias_map, w_out, tm, seq)
        return _ffn(s2, mod_f, mod_map(tm_f), row1(norm_g[0, 1]), f1, f3, f2, tm_f, tf)

    m0a, m0f = mods[0, :, 0:3], mods[0, :, 3:6]
    x2 = layer0(x2, t, tm_x, m0a, m0f, xmap, tm_x)
    c2 = layer0(c2, tc, tm_c, m0a, m0f, lambda tm: cmap, tm_c)

    m1a, m1f = mods[1, :, 0:3], mods[1, :, 3:6]
    win, wa, wukv = _mla_weights(o_w_in[0], o_w_uq[0], o_w_ukv[0])
    cos, sin = _rope_tables(t)
    ones = jnp.concatenate([jnp.ones((tc, QK_ROPE), F32), jnp.zeros((tc, LANE - QK_ROPE), F32)], axis=1)
    zeros = jnp.zeros((tc, LANE), F32)
    g10 = row1(norm_g[1, 0])
    proj = functools.partial(_mla_proj, g=g10, win=win, gq=row1(o_g_q[0]),
                             gkv=row1(o_g_kv[0]), wa=wa, wukv=wukv)
    q_x, k_x, v_x = proj(x2.reshape(b, t, d), m1a, xmap(tm_x), cos=cos, sin=sin, tm=tm_x)
    _, k_c, v_c = proj(c2.reshape(b, tc, d), m1a, cmap, cos=ones, sin=zeros, tm=tm_c)
    att = _attention(q_x, k_c, v_c, k_x, v_x, tq=min(4096, t), tk=min(256, t))

    r_pad = jnp.concatenate([o_router[0], jnp.zeros((d, LANE - N_EXPERTS), F32)], axis=1)
    r_hi = r_pad.astype(BF16)
    r_lo = (r_pad - r_hi.astype(F32)).astype(BF16)
    tm_o = min(512, t)
    x2, hp, sel = _outproj_router(att.reshape(n, -1), o_w_o[0].astype(BF16), x2, m1a, m1f, xmap(tm_o),
                                  row1(norm_g[1, 1]), r_hi, r_lo, tm_o)
    tm_e = 512
    pos, pad_slots, tile_expert, n_used, n_tiles = _routing_tables(sel, tm_e)
    tm_s = min(2048, t)
    hs = _scatter_rows(pos.reshape(n // tm_s, 1, 2 * tm_s), pad_slots, hp, n_tiles * tm_e, tm_s)
    dexp = o_exp_w1.shape[-1]
    tf_e = 2 * LANE
    assert dexp % tf_e == 0
    ys = _moe_ffn(tile_expert, n_used, hs, o_exp_w1[0].astype(BF16), o_exp_w3[0].astype(BF16),
                  o_exp_w2[0].astype(BF16), tm_e, tf_e)
    tm_k = min(512, t)
    out = _combine(pos.reshape(n // tm_k, 1, 2 * tm_k), x2, m1f, xmap(tm_k), sel, row1(final_g), ys, tm_k)
    return out.reshape(b, t, d)
```
